```python
import jax, jax.numpy as jnp
from jax import lax
import numpy as np

D_MODEL = 1024
BATCH = 2
SEQ = 8192
DEPTH = 4
DEC_BATCH = 128
DEC_SEQ = 8
PAST_LEN = 2048
PAGE_SIZE = 128

N_A = DEPTH // 2
MIX_W = 3 * D_MODEL // 4
MEM_HEADS = 4
MEM_HD = (D_MODEL - MIX_W) // MEM_HEADS
MEM_W = MEM_HEADS * MEM_HD
N_MEM = 256
POOL_WINDOWS = (2, 4, 8, 16)
POOL_GROUPS = len(POOL_WINDOWS)
POOL_GW = MIX_W // POOL_GROUPS
POOL_STATE = max(POOL_WINDOWS) - 1
B_HEADS = 8
B_HD = MIX_W // B_HEADS
DIL_PATTERNS = ((128, 1), (512, 4), (2048, 16))
WINDOW_MAX = max(w for w, _ in DIL_PATTERNS)
PEER_HEADS = 8
PEER_NKEYS = 128
PEER_EXPERTS = PEER_NKEYS * PEER_NKEYS
PEER_DK = 256
PEER_TOPK = 16
PEER_BLOCK = 256
EPS = 1e-6
NEG = -1e30

kernel_name = 'yoco_pool_dilated_peer_step'


def _rmsnorm(x, g):
    xf = x.astype(jnp.float32)
    r = lax.rsqrt(jnp.mean(xf * xf, axis=-1, keepdims=True) + EPS)
    return (xf * r).astype(x.dtype) * g


def _alibi_slopes(n):
    return jnp.asarray([2.0 ** (-8.0 * (h + 1) / n) for h in range(n)], dtype=jnp.float32)


def _softmax_stats(s):
    m = jnp.max(s, axis=-1, keepdims=True)
    e = jnp.exp(s - m)
    den = jnp.sum(e, axis=-1, keepdims=True)
    return e / den, (m + jnp.log(den))[..., 0]


def _pool_mix(z_cat, pos, n_prefix, w_group, scale):
    L = z_cat.shape[1]
    cs = jnp.cumsum(z_cat.astype(jnp.float32), axis=1)
    outs = []
    for g, w in enumerate(POOL_WINDOWS):
        c = cs[..., g * POOL_GW:(g + 1) * POOL_GW]
        shifted = jnp.pad(c, ((0, 0), (w, 0), (0, 0)))[:, :L]
        win_sum = (c - shifted)[:, n_prefix:]
        cnt = jnp.minimum(w, pos + 1).astype(jnp.float32)[None, :, None]
        zg = z_cat[:, n_prefix:, g * POOL_GW:(g + 1) * POOL_GW]
        outs.append(((win_sum / cnt).astype(z_cat.dtype) - zg) @ w_group[g])
    return jnp.concatenate(outs, axis=-1) * scale


def _mem_attend(q, mk, mv):
    s = jnp.einsum('nthd,nmhd->nhtm', q, mk).astype(jnp.float32) * (MEM_HD ** -0.5)
    p = jax.nn.softmax(s, axis=-1).astype(mv.dtype)
    o = jnp.einsum('nhtm,nmhd->nthd', p, mv)
    return o.reshape(q.shape[0], q.shape[1], MEM_W)


def _combine_groups(outs, lses, dtype):
    alpha = jax.nn.softmax(jnp.stack(lses, axis=0), axis=0)
    o = jnp.einsum('gnth,gnthd->nthd', alpha, jnp.stack(outs, axis=0).astype(jnp.float32))
    return o.reshape(o.shape[0], o.shape[1], MIX_W).astype(dtype)


def _dilated_prompt(q, k, v, slopes):
    N, S, H, Dh = q.shape
    outs, lses = [], []
    for w, d in DIL_PATTERNS:
        n = w // d
        L = S // d
        nb = -(-L // n)
        Lp = nb * n

        def to_blocks(a):
            a = a.reshape(N, L, d, H, Dh).transpose(0, 2, 1, 3, 4)
            a = jnp.pad(a, ((0, 0), (0, 0), (0, Lp - L), (0, 0), (0, 0)))
            return a.reshape(N, d, nb, n, H, Dh)

        def with_prev(a):
            prev = jnp.pad(a, ((0, 0), (0, 0), (1, 0), (0, 0), (0, 0), (0, 0)))[:, :, :nb]
            return jnp.concatenate([prev, a], axis=3)

        qb = to_blocks(q)
        kk = with_prev(to_blocks(k))
        vv = with_prev(to_blocks(v))
        a_idx = jnp.arange(n)[:, None]
        c_idx = jnp.arange(2 * n)[None, :]
        delta = n + a_idx - c_idx
        blk = jnp.arange(nb)[:, None, None]
        valid = (delta >= 0) & (delta <= n) & (blk * n + a_idx - delta >= 0)
        bias = -slopes[:, None, None] * (delta * d).astype(jnp.float32)[None]
        s = jnp.einsum('brcqhd,brckhd->brchqk', qb, kk).astype(jnp.float32) * (Dh ** -0.5) + bias
        s = jnp.where(valid[None, None, :, None], s, NEG)
        p, lse = _softmax_stats(s)
        o = jnp.einsum('brchqk,brckhd->brcqhd', p.astype(v.dtype), vv)
        o = o.reshape(N, d, Lp, H, Dh)[:, :, :L].transpose(0, 2, 1, 3, 4).reshape(N, S, H, Dh)
        lse = lse.transpose(0, 1, 2, 4, 3).reshape(N, d, Lp, H)[:, :, :L].transpose(0, 2, 1, 3).reshape(N, S, H)
        outs.append(o)
        lses.append(lse)
    return _combine_groups(outs, lses, q.dtype)


def _dilated_sample(q, k_cat, v_cat, slopes):
    N, T, H, Dh = q.shape
    P = k_cat.shape[1] - T
    outs, lses = [], []
    for w, d in DIL_PATTERNS:
        n = w // d
        steps = jnp.arange(n + 1)
        idx = P + jnp.arange(T)[:, None] - steps[None, :] * d
        valid = idx >= 0
        idx = jnp.maximum(idx, 0)
        kg = k_cat[:, idx]
        vg = v_cat[:, idx]
        bias = -slopes[:, None, None] * (steps * d).astype(jnp.float32)[None, None, :]
        s = jnp.einsum('bthd,btkhd->bhtk', q, kg).astype(jnp.float32) * (Dh ** -0.5) + bias
        s = jnp.where(valid[None, None], s, NEG)
        p, lse = _softmax_stats(s)
        o = jnp.einsum('bhtk,btkhd->bthd', p.astype(v_cat.dtype), vg)
        outs.append(o)
        lses.append(lse.transpose(0, 2, 1))
    return _combine_groups(outs, lses, q.dtype)


def _peer_block(h, wq, sub_keys, u, v):
    nt = h.shape[0]
    q = (h @ wq).reshape(nt, PEER_HEADS, 2, PEER_DK // 2)
    s = jnp.einsum('nhpd,pkd->nhpk', q, sub_keys).astype(jnp.float32)
    s1, i1 = lax.top_k(s[:, :, 0], PEER_TOPK)
    s2, i2 = lax.top_k(s[:, :, 1], PEER_TOPK)
    cand = (s1[..., :, None] + s2[..., None, :]).reshape(nt, PEER_HEADS, PEER_TOPK * PEER_TOPK)
    cidx = (i1[..., :, None] * PEER_NKEYS + i2[..., None, :]).reshape(nt, PEER_HEADS, PEER_TOPK * PEER_TOPK)
    top, sel = lax.top_k(cand, PEER_TOPK)
    e = jnp.take_along_axis(cidx, sel, axis=-1)
    g = jax.nn.softmax(top, axis=-1)
    act = jax.nn.gelu(jnp.einsum('nd,nhkd->nhk', h, u[e]).astype(jnp.float32), approximate=False)
    return jnp.einsum('nhk,nhkd->nd', (g * act).astype(h.dtype), v[e])


def _peer(x, wq, sub_keys, u, v):
    N, T, D = x.shape
    nt = N * T
    nblk = -(-nt // PEER_BLOCK)
    h = jnp.pad(x.reshape(nt, D), ((0, nblk * PEER_BLOCK - nt), (0, 0))).reshape(nblk, PEER_BLOCK, D)
    out = lax.map(lambda hb: _peer_block(hb, wq, sub_keys, u, v), h)
    return out.reshape(nblk * PEER_BLOCK, D)[:nt].reshape(N, T, D)


def _trunk(x, pos0, mem_k, mem_v, pool_prev, kv_prev, params):
    (norm_mix, w_in, pool_w, pool_scale, w_out, norm_kv, w_kv,
     norm_ffn, peer_wq, peer_subkeys, peer_u, peer_v, norm_final) = params
    n, t, _ = x.shape
    pos = pos0 + jnp.arange(t, dtype=jnp.int32)
    slopes = _alibi_slopes(B_HEADS)
    new_pool = []
    k_sh = v_sh = k_all = v_all = None
    for l in range(DEPTH):
        h = _rmsnorm(x, norm_mix[l])
        proj = h @ w_in[l]
        q_mix = proj[..., :MIX_W]
        q_mem = proj[..., MIX_W:].reshape(n, t, MEM_HEADS, MEM_HD)
        if l < N_A:
            if pool_prev is None:
                z_cat, n_prefix = q_mix, 0
            else:
                z_cat, n_prefix = jnp.concatenate([pool_prev[l], q_mix], axis=1), POOL_STATE
            new_pool.append(z_cat[:, -POOL_STATE:])
            mix = _pool_mix(z_cat, pos, n_prefix, pool_w[l], pool_scale[l])
        else:
            qh = q_mix.reshape(n, t, B_HEADS, B_HD)
            if kv_prev is None:
                mix = _dilated_prompt(qh, k_sh, v_sh, slopes)
            else:
                mix = _dilated_sample(qh, k_all, v_all, slopes)
        mem_o = _mem_attend(q_mem, mem_k[l], mem_v[l])
        x = x + jnp.concatenate([mix, mem_o], axis=-1) @ w_out[l]
        x = x + _peer(_rmsnorm(x, norm_ffn[l]), peer_wq[l], peer_subkeys[l], peer_u[l], peer_v[l])
        if l == N_A - 1:
            kv = _rmsnorm(x, norm_kv) @ w_kv
            k_sh = kv[..., :MIX_W].reshape(n, t, B_HEADS, B_HD)
            v_sh = kv[..., MIX_W:].reshape(n, t, B_HEADS, B_HD)
            if kv_prev is not None:
                k_all = jnp.concatenate([kv_prev[0], k_sh], axis=1)
                v_all = jnp.concatenate([kv_prev[1], v_sh], axis=1)
    return _rmsnorm(x, norm_final), jnp.stack(new_pool, axis=0), k_sh, v_sh


def setup_inputs(seed: int = 0) -> dict:
    key = jax.random.key(seed)
    ks = jax.random.split(key, 24)
    f32 = jnp.float32

    def nrm(k, shape, scale):
        return jax.random.normal(k, shape, f32) * scale

    def gain(k, shape):
        return 1.0 + 0.05 * jax.random.normal(k, shape, f32)

    w_buf = min(WINDOW_MAX, PAST_LEN)
    return {
        'x_prompt': nrm(ks[0], (BATCH, SEQ, D_MODEL), 1.0),
        'x_sample': nrm(ks[1], (DEC_BATCH, DEC_SEQ, D_MODEL), 1.0),
        'state_pool': nrm(ks[2], (N_A, DEC_BATCH, POOL_STATE, MIX_W), 1.0),
        'cache_win_k': nrm(ks[3], (DEC_BATCH, w_buf, B_HEADS, B_HD), 1.0),
        'cache_win_v': nrm(ks[4], (DEC_BATCH, w_buf, B_HEADS, B_HD), 1.0),
        'cache_mem_k': nrm(ks[5], (DEPTH, DEC_BATCH, N_MEM, MEM_HEADS, MEM_HD), 1.0),
        'cache_mem_v': nrm(ks[6], (DEPTH, DEC_BATCH, N_MEM, MEM_HEADS, MEM_HD), 1.0),
        'mem_prompt': nrm(ks[7], (BATCH, N_MEM, D_MODEL), 1.0),
        'norm_mix': gain(ks[8], (DEPTH, D_MODEL)),
        'w_in': nrm(ks[9], (DEPTH, D_MODEL, MIX_W + MEM_W), D_MODEL ** -0.5),
        'pool_w': nrm(ks[10], (N_A, POOL_GROUPS, POOL_GW, POOL_GW), POOL_GW ** -0.5),
        'pool_scale': 0.5 + 0.1 * jax.random.normal(ks[11], (N_A, MIX_W), f32),
        'norm_mem': gain(ks[12], (DEPTH, D_MODEL)),
        'w_mem_kv': nrm(ks[13], (DEPTH, D_MODEL, 2 * MEM_W), D_MODEL ** -0.5),
        'w_out': nrm(ks[14], (DEPTH, MIX_W + MEM_W, D_MODEL), 0.5 * (MIX_W + MEM_W) ** -0.5),
        'norm_kv': gain(ks[15], (D_MODEL,)),
        'w_kv': nrm(ks[16], (D_MODEL, 2 * MIX_W), D_MODEL ** -0.5),
        'norm_ffn': gain(ks[17], (DEPTH, D_MODEL)),
        'peer_wq': nrm(ks[18], (DEPTH, D_MODEL, PEER_HEADS * PEER_DK), D_MODEL ** -0.5),
        'peer_subkeys': nrm(ks[19], (DEPTH, 2, PEER_NKEYS, PEER_DK // 2), (PEER_DK // 2) ** -0.5),
        'peer_u': nrm(ks[20], (DEPTH, PEER_EXPERTS, D_MODEL), D_MODEL ** -0.5),
        'peer_v': nrm(ks[21], (DEPTH, PEER_EXPERTS, D_MODEL), 0.2),
        'norm_final': gain(ks[22], (D_MODEL,)),
    }


def reference(x_prompt, x_sample, state_pool, cache_win_k, cache_win_v, cache_mem_k, cache_mem_v,
              mem_prompt, norm_mix, w_in, pool_w, pool_scale, norm_mem, w_mem_kv, w_out,
              norm_kv, w_kv, norm_ffn, peer_wq, peer_subkeys, peer_u, peer_v, norm_final):
    params = (norm_mix, w_in, pool_w, pool_scale, w_out, norm_kv, w_kv,
              norm_ffn, peer_wq, peer_subkeys, peer_u, peer_v, norm_final)
    nb_p, n_mem, _ = mem_prompt.shape
    mk_list, mv_list = [], []
    for l in range(DEPTH):
        mkv = _rmsnorm(mem_prompt, norm_mem[l]) @ w_mem_kv[l]
        mk_list.append(mkv[..., :MEM_W].reshape(nb_p, n_mem, MEM_HEADS, MEM_HD))
        mv_list.append(mkv[..., MEM_W:].reshape(nb_p, n_mem, MEM_HEADS, MEM_HD))
    mem_k_p = jnp.stack(mk_list, axis=0)
    mem_v_p = jnp.stack(mv_list, axis=0)
    y_prompt, pool_p, k_p, v_p = _trunk(x_prompt, 0, mem_k_p, mem_v_p, None, None, params)
    keep = min(WINDOW_MAX, x_prompt.shape[1])
    win_k_p = k_p[:, -keep:]
    win_v_p = v_p[:, -keep:]
    y_sample, pool_s, k_s, v_s = _trunk(x_sample, PAST_LEN, cache_mem_k, cache_mem_v, state_pool,
                                        (cache_win_k, cache_win_v), params)
    return (y_prompt, y_sample, pool_p, win_k_p, win_v_p, mem_k_p, mem_v_p, pool_s, k_s, v_s)
```

```python
import functools
import math

import jax
import jax.numpy as jnp
from jax import lax
from jax.experimental import pallas as pl
from jax.experimental.pallas import tpu as pltpu

F32 = jnp.float32
BF16 = jnp.bfloat16

D_MODEL = 1024
DEPTH = 4
N_A = DEPTH // 2
MIX_W = 3 * D_MODEL // 4
MEM_HEADS = 4
MEM_HD = (D_MODEL - MIX_W) // MEM_HEADS
MEM_W = MEM_HEADS * MEM_HD
POOL_WINDOWS = (2, 4, 8, 16)
POOL_GROUPS = len(POOL_WINDOWS)
POOL_GW = MIX_W // POOL_GROUPS
POOL_STATE = max(POOL_WINDOWS) - 1
B_HEADS = 8
B_HD = MIX_W // B_HEADS
DIL_PATTERNS = ((128, 1), (512, 4), (2048, 16))
WINDOW_MAX = max(w for w, _ in DIL_PATTERNS)
PEER_HEADS = 8
PEER_NKEYS = 128
PEER_EXPERTS = PEER_NKEYS * PEER_NKEYS
PEER_DK = 256
PEER_TOPK = 16
PAST_LEN = 2048
EPS = 1e-6
NEG = -1e30

LANES = 128
SUBLANES = 8
TOK_TILE = 512
PEER_TOK = 256
PEER_CHUNK_KEYS = 16
PEER_CHUNK = PEER_CHUNK_KEYS * PEER_NKEYS
VMEM_LIMIT = 56 * 1024 * 1024


def _rms(x, g):
    r = lax.rsqrt(jnp.mean(x * x, axis=-1, keepdims=True) + EPS)
    return (x * r) * g


def _norm_matmul_kernel(x_ref, g_ref, w_ref, o_ref):
    h = _rms(x_ref[...], g_ref[...])
    o_ref[...] = jnp.dot(h.astype(BF16), w_ref[...], preferred_element_type=F32)


def _norm_matmul(x, g, w_bf16, tile=TOK_TILE):
    t, d = x.shape
    n = w_bf16.shape[1]
    tile = min(tile, t)
    assert t % tile == 0
    return pl.pallas_call(
        _norm_matmul_kernel,
        grid=(t // tile,),
        in_specs=[
            pl.BlockSpec((tile, d), lambda i: (i, 0)),
            pl.BlockSpec((1, d), lambda i: (0, 0)),
            pl.BlockSpec((d, n), lambda i: (0, 0)),
        ],
        out_specs=pl.BlockSpec((tile, n), lambda i: (i, 0)),
        out_shape=jax.ShapeDtypeStruct((t, n), F32),
        compiler_params=pltpu.CompilerParams(
            dimension_semantics=("parallel",), vmem_limit_bytes=VMEM_LIMIT),
        name="norm_matmul",
    )(x, g.reshape(1, d), w_bf16)


def _proj_residual_kernel(x_ref, a_ref, w_ref, o_ref):
    o_ref[...] = x_ref[...] + jnp.dot(a_ref[...].astype(BF16), w_ref[...],
                                      preferred_element_type=F32)


def _proj_residual(x, a, w_bf16, tile=TOK_TILE):
    t, d = x.shape
    k = a.shape[1]
    assert t % tile == 0
    return pl.pallas_call(
        _proj_residual_kernel,
        grid=(t // tile,),
        in_specs=[
            pl.BlockSpec((tile, d), lambda i: (i, 0)),
            pl.BlockSpec((tile, k), lambda i: (i, 0)),
            pl.BlockSpec((k, d), lambda i: (0, 0)),
        ],
        out_specs=pl.BlockSpec((tile, d), lambda i: (i, 0)),
        out_shape=jax.ShapeDtypeStruct((t, d), F32),
        compiler_params=pltpu.CompilerParams(
            dimension_semantics=("parallel",), vmem_limit_bytes=VMEM_LIMIT),
        name="proj_residual",
    )(x, a, w_bf16)


def _final_norm_kernel(x_ref, g_ref, o_ref):
    o_ref[...] = _rms(x_ref[...], g_ref[...])


def _final_norm(x, g, tile=TOK_TILE):
    t, d = x.shape
    assert t % tile == 0
    return pl.pallas_call(
        _final_norm_kernel,
        grid=(t // tile,),
        in_specs=[
            pl.BlockSpec((tile, d), lambda i: (i, 0)),
            pl.BlockSpec((1, d), lambda i: (0, 0)),
        ],
        out_specs=pl.BlockSpec((tile, d), lambda i: (i, 0)),
        out_shape=jax.ShapeDtypeStruct((t, d), F32),
        compiler_params=pltpu.CompilerParams(dimension_semantics=("parallel",)),
        name="final_norm",
    )(x, g.reshape(1, d))


def _extract_top(s, out_ref, n):
    for k in range(n):
        m = jnp.max(s, axis=0, keepdims=True)
        out_ref[k:k + 1, :] = m
        s = jnp.where(s == m, -jnp.inf, s)


def _peer_kernel(x_ref, g_ref, wqt_ref, sk_ref, u_ref, vt_ref, o_ref,
                 ht_s, q_s, s1_s, s2_s, e1_s, e2_s, s1r_s, e1r_s, tau_s, a_s, b_s, cand_s, top_s,
                 sc_s, w_s, acc_s):
    c = pl.program_id(1)
    tok = x_ref.shape[0]

    @pl.when(c == 0)
    def _route():
        h = _rms(x_ref[...], g_ref[...])
        ht_s[...] = h.T.astype(BF16)
        q_s[...] = jnp.dot(wqt_ref[...], ht_s[...],
                           preferred_element_type=F32).astype(BF16)
        acc_s[...] = jnp.zeros_like(acc_s)

        def head_body(hd, carry):
            base = pl.multiple_of(hd * PEER_DK, PEER_DK)
            half = PEER_DK // 2
            s1 = jnp.dot(sk_ref[0], q_s[pl.ds(base, half), :], preferred_element_type=F32)
            s2 = jnp.dot(sk_ref[1], q_s[pl.ds(base + half, half), :],
                         preferred_element_type=F32)
            s1_s[hd] = s1
            s2_s[hd] = s2
            for lt in range(tok // LANES):
                ln = slice(lt * LANES, (lt + 1) * LANES)
                _extract_top(s1[:, ln], a_s, PEER_TOPK)
                _extract_top(s2[:, ln], b_s, PEER_TOPK)
                b = b_s[...]
                for k in range(PEER_TOPK):
                    cand_s[k * PEER_TOPK:(k + 1) * PEER_TOPK, :] = a_s[k:k + 1, :] + b
                _extract_top(cand_s[...], top_s, PEER_TOPK)
                top = top_s[...]
                m0 = top[0:1, :]
                den = jnp.sum(jnp.exp(top - m0), axis=0, keepdims=True)
                tau_s[hd, :, ln] = top[PEER_TOPK - 1:PEER_TOPK, :]
                e1_s[hd, :, ln] = jnp.exp(s1[:, ln] - a_s[0:1, :]) / den
                e2_s[hd, :, ln] = jnp.exp(s2[:, ln] - b_s[0:1, :])
            return carry

        lax.fori_loop(0, PEER_HEADS, head_body, 0)

        for hd in range(PEER_HEADS):
            for r in range(PEER_NKEYS // SUBLANES):
                rows = slice(r * SUBLANES, (r + 1) * SUBLANES)
                s1_blk = s1_s[hd, rows, :]
                e1_blk = e1_s[hd, rows, :]
                for k in range(SUBLANES):
                    s1r_s[r * SUBLANES + k, hd:hd + 1, :] = s1_blk[k:k + 1, :]
                    e1r_s[r * SUBLANES + k, hd:hd + 1, :] = e1_blk[k:k + 1, :]

    sc_s[...] = jnp.dot(u_ref[...], ht_s[...], preferred_element_type=F32)

    def key_body(j, carry):
        i1 = c * PEER_CHUNK_KEYS + j
        row0 = pl.multiple_of(j * PEER_NKEYS, PEER_NKEYS)
        for lt in range(tok // LANES):
            ln = slice(lt * LANES, (lt + 1) * LANES)
            wgt = jnp.zeros((PEER_NKEYS, LANES), F32)
            s1_rows = s1r_s[i1, :, ln]
            e1_rows = e1r_s[i1, :, ln]
            for hd in range(PEER_HEADS):
                t = s1_rows[hd:hd + 1, :] + s2_s[hd, :, ln]
                p = e1_rows[hd:hd + 1, :] * e2_s[hd, :, ln]
                wgt = wgt + jnp.where(t >= tau_s[hd, :, ln], p, 0.0)
            pre = sc_s[pl.ds(row0, PEER_NKEYS), ln]
            act = 0.5 * pre * (1.0 + lax.erf(pre * (1.0 / math.sqrt(2.0))))
            w_s[pl.ds(row0, PEER_NKEYS), ln] = (wgt * act).astype(BF16)
        return carry

    lax.fori_loop(0, PEER_CHUNK_KEYS, key_body, 0)
    acc_s[...] += jnp.dot(vt_ref[...], w_s[...], preferred_element_type=F32)

    @pl.when(c == pl.num_programs(1) - 1)
    def _finish():
        o_ref[...] = x_ref[...] + acc_s[...].T


def _peer(x, g, wqt_bf16, sk_bf16, u_bf16, vt_bf16, tok=PEER_TOK):
    t, d = x.shape
    assert t % tok == 0
    n_chunks = PEER_EXPERTS // PEER_CHUNK
    hq = PEER_HEADS * PEER_DK
    return pl.pallas_call(
        _peer_kernel,
        grid=(t // tok, n_chunks),
        in_specs=[
            pl.BlockSpec((tok, d), lambda i, c: (i, 0)),
            pl.BlockSpec((1, d), lambda i, c: (0, 0)),
            pl.BlockSpec((hq, d), lambda i, c: (0, 0)),
            pl.BlockSpec((2, PEER_NKEYS, PEER_DK // 2), lambda i, c: (0, 0, 0)),
            pl.BlockSpec((PEER_CHUNK, d), lambda i, c: (c, 0)),
            pl.BlockSpec((d, PEER_CHUNK), lambda i, c: (0, c)),
        ],
        out_specs=pl.BlockSpec((tok, d), lambda i, c: (i, 0)),
        out_shape=jax.ShapeDtypeStruct((t, d), F32),
        scratch_shapes=[
            pltpu.VMEM((d, tok), BF16),
            pltpu.VMEM((hq, tok), BF16),
            pltpu.VMEM((PEER_HEADS, PEER_NKEYS, tok), F32),
            pltpu.VMEM((PEER_HEADS, PEER_NKEYS, tok), F32),
            pltpu.VMEM((PEER_HEADS, PEER_NKEYS, tok), F32),
            pltpu.VMEM((PEER_HEADS, PEER_NKEYS, tok), F32),
            pltpu.VMEM((PEER_NKEYS, PEER_HEADS, tok), F32),
            pltpu.VMEM((PEER_NKEYS, PEER_HEADS, tok), F32),
            pltpu.VMEM((PEER_HEADS, 1, tok), F32),
            pltpu.VMEM((PEER_TOPK, LANES), F32),
            pltpu.VMEM((PEER_TOPK, LANES), F32),
            pltpu.VMEM((PEER_TOPK * PEER_TOPK, LANES), F32),
            pltpu.VMEM((PEER_TOPK, LANES), F32),
            pltpu.VMEM((PEER_CHUNK, tok), F32),
            pltpu.VMEM((PEER_CHUNK, tok), BF16),
            pltpu.VMEM((d, tok), F32),
        ],
        compiler_params=pltpu.CompilerParams(
            dimension_semantics=("parallel", "arbitrary"), vmem_limit_bytes=VMEM_LIMIT),
        name="peer",
    )(x, g.reshape(1, d), wqt_bf16, sk_bf16, u_bf16, vt_bf16)


def _alibi_slopes(n):
    return jnp.asarray([2.0 ** (-8.0 * (h + 1) / n) for h in range(n)], dtype=F32)


def _softmax_stats(s):
    m = jnp.max(s, axis=-1, keepdims=True)
    e = jnp.exp(s - m)
    den = jnp.sum(e, axis=-1, keepdims=True)
    return e / den, (m + jnp.log(den))[..., 0]


def _pool_mix(z_cat, pos, n_prefix, w_group, scale):
    L = z_cat.shape[1]
    outs = []
    for g, w in enumerate(POOL_WINDOWS):
        zc = z_cat[..., g * POOL_GW:(g + 1) * POOL_GW]
        acc = zc
        for j in range(1, w):
            acc = acc + jnp.pad(zc, ((0, 0), (j, 0), (0, 0)))[:, :L]
        win_sum = acc[:, n_prefix:]
        cnt = jnp.minimum(w, pos + 1).astype(F32)[None, :, None]
        zg = zc[:, n_prefix:]
        outs.append(((win_sum / cnt) - zg) @ w_group[g])
    return jnp.concatenate(outs, axis=-1) * scale


def _mem_attend(q, mk, mv):
    s = jnp.einsum('nthd,nmhd->nhtm', q, mk).astype(F32) * (MEM_HD ** -0.5)
    p = jax.nn.softmax(s, axis=-1)
    o = jnp.einsum('nhtm,nmhd->nthd', p, mv)
    return o.reshape(q.shape[0], q.shape[1], MEM_W)


def _combine_groups(outs, lses, dtype):
    alpha = jax.nn.softmax(jnp.stack(lses, axis=0), axis=0)
    o = jnp.einsum('gnth,gnthd->nthd', alpha, jnp.stack(outs, axis=0).astype(F32))
    return o.reshape(o.shape[0], o.shape[1], MIX_W).astype(dtype)


def _dilated_prompt(q, k, v, slopes):
    N, S, H, Dh = q.shape
    outs, lses = [], []
    for w, d in DIL_PATTERNS:
        n = w // d
        L = S // d
        nb = -(-L // n)
        Lp = nb * n

        def to_blocks(a):
            a = a.reshape(N, L, d, H, Dh).transpose(0, 2, 1, 3, 4)
            a = jnp.pad(a, ((0, 0), (0, 0), (0, Lp - L), (0, 0), (0, 0)))
            return a.reshape(N, d, nb, n, H, Dh)

        def with_prev(a):
            prev = jnp.pad(a, ((0, 0), (0, 0), (1, 0), (0, 0), (0, 0), (0, 0)))[:, :, :nb]
            return jnp.concatenate([prev, a], axis=3)

        qb = to_blocks(q)
        kk = with_prev(to_blocks(k))
        vv = with_prev(to_blocks(v))
        a_idx = jnp.arange(n)[:, None]
        c_idx = jnp.arange(2 * n)[None, :]
        delta = n + a_idx - c_idx
        blk = jnp.arange(nb)[:, None, None]
        valid = (delta >= 0) & (delta <= n) & (blk * n + a_idx - delta >= 0)
        bias = -slopes[:, None, None] * (delta * d).astype(F32)[None]
        s = jnp.einsum('brcqhd,brckhd->brchqk', qb, kk).astype(F32) * (Dh ** -0.5) + bias
        s = jnp.where(valid[None, None, :, None], s, NEG)
        p, lse = _softmax_stats(s)
        o = jnp.einsum('brchqk,brckhd->brcqhd', p, vv)
        o = o.reshape(N, d, Lp, H, Dh)[:, :, :L].transpose(0, 2, 1, 3, 4).reshape(N, S, H, Dh)
        lse = lse.transpose(0, 1, 2, 4, 3).reshape(N, d, Lp, H)[:, :, :L].transpose(0, 2, 1, 3).reshape(N, S, H)
        outs.append(o)
        lses.append(lse)
    return _combine_groups(outs, lses, q.dtype)


def _dilated_sample(q, k_cat, v_cat, slopes):
    N, T, H, Dh = q.shape
    P = k_cat.shape[1] - T
    outs, lses = [], []
    for w, d in DIL_PATTERNS:
        n = w // d
        steps = jnp.arange(n + 1)
        idx = P + jnp.arange(T)[:, None] - steps[None, :] * d
        valid = idx >= 0
        idx = jnp.maximum(idx, 0)
        kg = k_cat[:, idx]
        vg = v_cat[:, idx]
        bias = -slopes[:, None, None] * (steps * d).astype(F32)[None, None, :]
        s = jnp.einsum('bthd,btkhd->bhtk', q, kg).astype(F32) * (Dh ** -0.5) + bias
        s = jnp.where(valid[None, None], s, NEG)
        p, lse = _softmax_stats(s)
        o = jnp.einsum('bhtk,btkhd->bthd', p, vg)
        outs.append(o)
        lses.append(lse.transpose(0, 2, 1))
    return _combine_groups(outs, lses, q.dtype)


def kernel(x_prompt, x_sample, state_pool, cache_win_k, cache_win_v, cache_mem_k, cache_mem_v,
           mem_prompt, norm_mix, w_in, pool_w, pool_scale, norm_mem, w_mem_kv, w_out,
           norm_kv, w_kv, norm_ffn, peer_wq, peer_subkeys, peer_u, peer_v, norm_final):
    nb_p, seq, d = x_prompt.shape
    nb_s, dec_seq, _ = x_sample.shape
    n_mem = mem_prompt.shape[1]
    tp = nb_p * seq
    ts = nb_s * dec_seq
    slopes = _alibi_slopes(B_HEADS)

    mem_flat = mem_prompt.reshape(nb_p * n_mem, d)
    mk_list, mv_list = [], []
    for l in range(DEPTH):
        mkv = _norm_matmul(mem_flat, norm_mem[l], w_mem_kv[l].astype(BF16))
        mk_list.append(mkv[:, :MEM_W].reshape(nb_p, n_mem, MEM_HEADS, MEM_HD))
        mv_list.append(mkv[:, MEM_W:].reshape(nb_p, n_mem, MEM_HEADS, MEM_HD))
    mem_k_p = jnp.stack(mk_list, axis=0)
    mem_v_p = jnp.stack(mv_list, axis=0)

    x = jnp.concatenate([x_prompt.reshape(tp, d), x_sample.reshape(ts, d)], axis=0)
    pos_p = jnp.arange(seq, dtype=jnp.int32)
    pos_s = PAST_LEN + jnp.arange(dec_seq, dtype=jnp.int32)
    pool_p, pool_s = [], []
    k_p = v_p = k_s = v_s = k_all = v_all = None
    for l in range(DEPTH):
        proj = _norm_matmul(x, norm_mix[l], w_in[l].astype(BF16))
        proj_p = proj[:tp].reshape(nb_p, seq, d)
        proj_s = proj[tp:].reshape(nb_s, dec_seq, d)
        if l < N_A:
            zp = proj_p[..., :MIX_W]
            pool_p.append(zp[:, -POOL_STATE:])
            mix_p = _pool_mix(zp, pos_p, 0, pool_w[l], pool_scale[l])
            zs = jnp.concatenate([state_pool[l], proj_s[..., :MIX_W]], axis=1)
            pool_s.append(zs[:, -POOL_STATE:])
            mix_s = _pool_mix(zs, pos_s, POOL_STATE, pool_w[l], pool_scale[l])
        else:
            qp = proj_p[..., :MIX_W].reshape(nb_p, seq, B_HEADS, B_HD)
            qs = proj_s[..., :MIX_W].reshape(nb_s, dec_seq, B_HEADS, B_HD)
            mix_p = _dilated_prompt(qp, k_p, v_p, slopes)
            mix_s = _dilated_sample(qs, k_all, v_all, slopes)
        mem_p = _mem_attend(proj_p[..., MIX_W:].reshape(nb_p, seq, MEM_HEADS, MEM_HD),
                            mem_k_p[l], mem_v_p[l])
        mem_s = _mem_attend(proj_s[..., MIX_W:].reshape(nb_s, dec_seq, MEM_HEADS, MEM_HD),
                            cache_mem_k[l], cache_mem_v[l])
        cat = jnp.concatenate([
            jnp.concatenate([mix_p, mem_p], axis=-1).reshape(tp, d),
            jnp.concatenate([mix_s, mem_s], axis=-1).reshape(ts, d)], axis=0)
        x = _proj_residual(x, cat, w_out[l].astype(BF16))
        x = _peer(x, norm_ffn[l], peer_wq[l].T.astype(BF16), peer_subkeys[l].astype(BF16),
                  peer_u[l].astype(BF16), peer_v[l].T.astype(BF16))
        if l == N_A - 1:
            kv = _norm_matmul(x, norm_kv, w_kv.astype(BF16))
            k_p = kv[:tp, :MIX_W].reshape(nb_p, seq, B_HEADS, B_HD)
            v_p = kv[:tp, MIX_W:].reshape(nb_p, seq, B_HEADS, B_HD)
            k_s = kv[tp:, :MIX_W].reshape(nb_s, dec_seq, B_HEADS, B_HD)
            v_s = kv[tp:, MIX_W:].reshape(nb_s, dec_seq, B_HEADS, B_HD)
            k_all = jnp.concatenate([cache_win_k, k_s], axis=1)
            v_all = jnp.concatenate([cache_win_v, v_s], axis=1)
    y = _final_norm(x, norm_final)
    keep = min(WINDOW_MAX, seq)
    return (y[:tp].reshape(nb_p, seq, d), y[tp:].reshape(nb_s, dec_seq, d),
            jnp.stack(pool_p, axis=0), k_p[:, -keep:], v_p[:, -keep:], mem_k_p, mem_v_p,
            jnp.stack(pool_s, axis=0), k_s, v_s)
```

```python
import functools
import math

import numpy as np
import jax
import jax.numpy as jnp
from jax import lax
from jax.experimental import pallas as pl
from jax.experimental.pallas import tpu as pltpu

F32 = jnp.float32
BF16 = jnp.bfloat16

D_MODEL = 1024
DEPTH = 4
N_A = DEPTH // 2
MIX_W = 3 * D_MODEL // 4
MEM_HEADS = 4
MEM_HD = (D_MODEL - MIX_W) // MEM_HEADS
MEM_W = MEM_HEADS * MEM_HD
POOL_WINDOWS = (2, 4, 8, 16)
POOL_GROUPS = len(POOL_WINDOWS)
POOL_GW = MIX_W // POOL_GROUPS
POOL_STATE = max(POOL_WINDOWS) - 1
B_HEADS = 8
B_HD = MIX_W // B_HEADS
DIL_PATTERNS = ((128, 1), (512, 4), (2048, 16))
WINDOW_MAX = max(w for w, _ in DIL_PATTERNS)
PEER_HEADS = 8
PEER_NKEYS = 128
PEER_EXPERTS = PEER_NKEYS * PEER_NKEYS
PEER_DK = 256
PEER_TOPK = 16
PAST_LEN = 2048
EPS = 1e-6
NEG = -1e30

LANES = 128
SUBLANES = 8
TOK_TILE = 512
PEER_TOK = 512
PEER_CHUNK_KEYS = 8
PEER_CHUNK = PEER_CHUNK_KEYS * PEER_NKEYS
POOL_HALO = 16
DIL_BLOCK = 128
DIL_WIN = 2048
SAMPLE_GROUP = 16
VMEM_LIMIT = 56 * 1024 * 1024


def _rms(x, g):
    r = lax.rsqrt(jnp.mean(x * x, axis=-1, keepdims=True) + EPS)
    return (x * r) * g


def _norm_matmul_kernel(x_ref, g_ref, w_ref, o_ref):
    h = _rms(x_ref[...], g_ref[...])
    o_ref[...] = jnp.dot(h.astype(BF16), w_ref[...], preferred_element_type=F32)


def _norm_matmul(x, g, w_bf16, tile=TOK_TILE):
    t, d = x.shape
    n = w_bf16.shape[1]
    tile = min(tile, t)
    assert t % tile == 0
    return pl.pallas_call(
        _norm_matmul_kernel,
        grid=(t // tile,),
        in_specs=[
            pl.BlockSpec((tile, d), lambda i: (i, 0)),
            pl.BlockSpec((1, d), lambda i: (0, 0)),
            pl.BlockSpec((d, n), lambda i: (0, 0)),
        ],
        out_specs=pl.BlockSpec((tile, n), lambda i: (i, 0)),
        out_shape=jax.ShapeDtypeStruct((t, n), F32),
        compiler_params=pltpu.CompilerParams(
            dimension_semantics=("parallel",), vmem_limit_bytes=VMEM_LIMIT),
        name="norm_matmul",
    )(x, g.reshape(1, d), w_bf16)


def _final_norm_kernel(x_ref, g_ref, o_ref):
    o_ref[...] = _rms(x_ref[...], g_ref[...])


def _final_norm(x, g, tile=TOK_TILE):
    t, d = x.shape
    assert t % tile == 0
    return pl.pallas_call(
        _final_norm_kernel,
        grid=(t // tile,),
        in_specs=[
            pl.BlockSpec((tile, d), lambda i: (i, 0)),
            pl.BlockSpec((1, d), lambda i: (0, 0)),
        ],
        out_specs=pl.BlockSpec((tile, d), lambda i: (i, 0)),
        out_shape=jax.ShapeDtypeStruct((t, d), F32),
        compiler_params=pltpu.CompilerParams(dimension_semantics=("parallel",)),
        name="final_norm",
    )(x, g.reshape(1, d))


def _extract_top(s, out_ref, n):
    for k in range(n):
        m = jnp.max(s, axis=0, keepdims=True)
        out_ref[k:k + 1, :] = m
        s = jnp.where(s == m, -jnp.inf, s)


def _peer_kernel(x_ref, g_ref, wqt_ref, sk_ref, u_ref, vt_ref, o_ref,
                 ht_s, q_s, s1_s, s2_s, e1_s, e2_s, s1r_s, e1r_s, tau_s, a_s, b_s, cand_s, top_s,
                 sc_s, w_s, acc_s):
    c = pl.program_id(1)
    tok = x_ref.shape[0]

    @pl.when(c == 0)
    def _route():
        h = _rms(x_ref[...], g_ref[...])
        ht_s[...] = h.T.astype(BF16)
        q_s[...] = jnp.dot(wqt_ref[...], ht_s[...],
                           preferred_element_type=F32).astype(BF16)
        acc_s[...] = jnp.zeros_like(acc_s)

        def head_body(hd, carry):
            base = pl.multiple_of(hd * PEER_DK, PEER_DK)
            half = PEER_DK // 2
            s1 = jnp.dot(sk_ref[0], q_s[pl.ds(base, half), :], preferred_element_type=F32)
            s2 = jnp.dot(sk_ref[1], q_s[pl.ds(base + half, half), :],
                         preferred_element_type=F32)
            s1_s[hd] = s1
            s2_s[hd] = s2
            for lt in range(tok // LANES):
                ln = slice(lt * LANES, (lt + 1) * LANES)
                _extract_top(s1[:, ln], a_s, PEER_TOPK)
                _extract_top(s2[:, ln], b_s, PEER_TOPK)
                b = b_s[...]
                for k in range(PEER_TOPK):
                    cand_s[k * PEER_TOPK:(k + 1) * PEER_TOPK, :] = a_s[k:k + 1, :] + b
                _extract_top(cand_s[...], top_s, PEER_TOPK)
                top = top_s[...]
                m0 = top[0:1, :]
                den = jnp.sum(jnp.exp(top - m0), axis=0, keepdims=True)
                tau_s[hd, :, ln] = top[PEER_TOPK - 1:PEER_TOPK, :]
                e1_s[hd, :, ln] = jnp.exp(s1[:, ln] - a_s[0:1, :]) / den
                e2_s[hd, :, ln] = jnp.exp(s2[:, ln] - b_s[0:1, :])
            return carry

        lax.fori_loop(0, PEER_HEADS, head_body, 0)

        for hd in range(PEER_HEADS):
            for r in range(PEER_NKEYS // SUBLANES):
                rows = slice(r * SUBLANES, (r + 1) * SUBLANES)
                s1_blk = s1_s[hd, rows, :]
                e1_blk = e1_s[hd, rows, :]
                for k in range(SUBLANES):
                    s1r_s[r * SUBLANES + k, hd:hd + 1, :] = s1_blk[k:k + 1, :]
                    e1r_s[r * SUBLANES + k, hd:hd + 1, :] = e1_blk[k:k + 1, :]

    sc_s[...] = jnp.dot(u_ref[...], ht_s[...], preferred_element_type=F32)

    def key_body(j, carry):
        i1 = c * PEER_CHUNK_KEYS + j
        row0 = pl.multiple_of(j * PEER_NKEYS, PEER_NKEYS)
        for lt in range(tok // LANES):
            ln = slice(lt * LANES, (lt + 1) * LANES)
            wgt = jnp.zeros((PEER_NKEYS, LANES), F32)
            s1_rows = s1r_s[i1, :, ln]
            e1_rows = e1r_s[i1, :, ln]
            for hd in range(PEER_HEADS):
                t = s1_rows[hd:hd + 1, :] + s2_s[hd, :, ln]
                p = e1_rows[hd:hd + 1, :] * e2_s[hd, :, ln]
                wgt = wgt + jnp.where(t >= tau_s[hd, :, ln], p, 0.0)
            pre = sc_s[pl.ds(row0, PEER_NKEYS), ln]
            act = 0.5 * pre * (1.0 + lax.erf(pre * (1.0 / math.sqrt(2.0))))
            w_s[pl.ds(row0, PEER_NKEYS), ln] = (wgt * act).astype(BF16)
        return carry

    lax.fori_loop(0, PEER_CHUNK_KEYS, key_body, 0)
    acc_s[...] += jnp.dot(vt_ref[...], w_s[...], preferred_element_type=F32)

    @pl.when(c == pl.num_programs(1) - 1)
    def _finish():
        o_ref[...] = x_ref[...] + acc_s[...].T


def _peer(x, g, wqt_bf16, sk_bf16, u_bf16, vt_bf16, tok=PEER_TOK):
    t, d = x.shape
    assert t % tok == 0
    n_chunks = PEER_EXPERTS // PEER_CHUNK
    hq = PEER_HEADS * PEER_DK
    return pl.pallas_call(
        _peer_kernel,
        grid=(t // tok, n_chunks),
        in_specs=[
            pl.BlockSpec((tok, d), lambda i, c: (i, 0)),
            pl.BlockSpec((1, d), lambda i, c: (0, 0)),
            pl.BlockSpec((hq, d), lambda i, c: (0, 0)),
            pl.BlockSpec((2, PEER_NKEYS, PEER_DK // 2), lambda i, c: (0, 0, 0)),
            pl.BlockSpec((PEER_CHUNK, d), lambda i, c: (c, 0)),
            pl.BlockSpec((d, PEER_CHUNK), lambda i, c: (0, c)),
        ],
        out_specs=pl.BlockSpec((tok, d), lambda i, c: (i, 0)),
        out_shape=jax.ShapeDtypeStruct((t, d), F32),
        scratch_shapes=[
            pltpu.VMEM((d, tok), BF16),
            pltpu.VMEM((hq, tok), BF16),
            pltpu.VMEM((PEER_HEADS, PEER_NKEYS, tok), F32),
            pltpu.VMEM((PEER_HEADS, PEER_NKEYS, tok), F32),
            pltpu.VMEM((PEER_HEADS, PEER_NKEYS, tok), F32),
            pltpu.VMEM((PEER_HEADS, PEER_NKEYS, tok), F32),
            pltpu.VMEM((PEER_NKEYS, PEER_HEADS, tok), F32),
            pltpu.VMEM((PEER_NKEYS, PEER_HEADS, tok), F32),
            pltpu.VMEM((PEER_HEADS, 1, tok), F32),
            pltpu.VMEM((PEER_TOPK, LANES), F32),
            pltpu.VMEM((PEER_TOPK, LANES), F32),
            pltpu.VMEM((PEER_TOPK * PEER_TOPK, LANES), F32),
            pltpu.VMEM((PEER_TOPK, LANES), F32),
            pltpu.VMEM((PEER_CHUNK, tok), F32),
            pltpu.VMEM((PEER_CHUNK, tok), BF16),
            pltpu.VMEM((d, tok), F32),
        ],
        compiler_params=pltpu.CompilerParams(
            dimension_semantics=("parallel", "arbitrary"), vmem_limit_bytes=VMEM_LIMIT),
        name="peer",
    )(x, g.reshape(1, d), wqt_bf16, sk_bf16, u_bf16, vt_bf16)


def _pool_tile_plan(ct):
    lo = ct * LANES
    hi = lo + LANES - 1
    return [(POOL_WINDOWS[g], (g + 1) * POOL_GW) for g in range(lo // POOL_GW, hi // POOL_GW + 1)]


def _pool_diff_tile(load_shifted, z_tile, pos1, ct):
    plan = _pool_tile_plan(ct)
    wanted = {w for w, _ in plan}
    acc = z_tile
    snaps = {}
    for j in range(1, max(wanted)):
        acc = acc + load_shifted(j)
        if j + 1 in wanted:
            snaps[j + 1] = acc
    if len(plan) == 1:
        w = plan[0][0]
        win = snaps[w]
        cnt = jnp.minimum(float(w), pos1)
    else:
        (w_lo, edge), (w_hi, _) = plan
        lane = lax.broadcasted_iota(jnp.int32, (1,) * (z_tile.ndim - 1) + (LANES,), z_tile.ndim - 1)
        in_lo = lane + ct * LANES < edge
        win = jnp.where(in_lo, snaps[w_lo], snaps[w_hi])
        cnt = jnp.minimum(jnp.where(in_lo, float(w_lo), float(w_hi)), pos1)
    return win / cnt - z_tile


def _pool_prompt_kernel(z_ref, wbd_ref, scale_ref, mix_ref, state_ref, zc_s, d_s):
    i = pl.program_id(1)
    tile = z_ref.shape[0]

    @pl.when(i == 0)
    def _start():
        zc_s[0:POOL_HALO, :] = jnp.zeros((POOL_HALO, MIX_W), F32)

    zc_s[POOL_HALO:POOL_HALO + tile, :] = z_ref[...]
    pos1 = (i * tile + lax.broadcasted_iota(jnp.int32, (tile, 1), 0) + 1).astype(F32)
    for ct in range(MIX_W // LANES):
        ln = slice(ct * LANES, (ct + 1) * LANES)
        diff = _pool_diff_tile(
            lambda j: zc_s[POOL_HALO - j:POOL_HALO - j + tile, ln], z_ref[:, ln], pos1, ct)
        d_s[:, ln] = diff.astype(BF16)
    mix_ref[...] = jnp.dot(d_s[...], wbd_ref[...], preferred_element_type=F32) * scale_ref[...]
    last = z_ref[tile - POOL_HALO:tile, :]
    state_ref[...] = last
    zc_s[0:POOL_HALO, :] = last


def _pool_prompt(proj, nb, seq, wbd_bf16, scale, tile=TOK_TILE):
    nt = seq // tile
    assert seq % tile == 0 and tile >= POOL_HALO
    return pl.pallas_call(
        _pool_prompt_kernel,
        grid=(nb, nt),
        in_specs=[
            pl.BlockSpec((tile, MIX_W), lambda n, i: (n * nt + i, 0)),
            pl.BlockSpec((MIX_W, MIX_W), lambda n, i: (0, 0)),
            pl.BlockSpec((1, MIX_W), lambda n, i: (0, 0)),
        ],
        out_specs=[
            pl.BlockSpec((tile, MIX_W), lambda n, i: (n * nt + i, 0)),
            pl.BlockSpec((None, POOL_HALO, MIX_W), lambda n, i: (n, 0, 0)),
        ],
        out_shape=[
            jax.ShapeDtypeStruct((nb * seq, MIX_W), F32),
            jax.ShapeDtypeStruct((nb, POOL_HALO, MIX_W), F32),
        ],
        scratch_shapes=[
            pltpu.VMEM((POOL_HALO + tile, MIX_W), F32),
            pltpu.VMEM((tile, MIX_W), BF16),
        ],
        compiler_params=pltpu.CompilerParams(
            dimension_semantics=("arbitrary", "arbitrary"), vmem_limit_bytes=VMEM_LIMIT),
        name="pool_prompt",
    )(proj, wbd_bf16, scale.reshape(1, MIX_W))


def _pool_sample_kernel(z_ref, st_ref, wbd_ref, scale_ref, mix_ref, nst_ref, zc_s, d_s, *, pos0):
    grp, n_state, _ = st_ref.shape
    t = z_ref.shape[0] // grp
    zc_s[:, POOL_HALO - n_state:POOL_HALO, :] = st_ref[...]
    zc_s[:, POOL_HALO:POOL_HALO + t, :] = z_ref[...].reshape(grp, t, MIX_W)
    pos1 = (pos0 + lax.broadcasted_iota(jnp.int32, (1, t, 1), 1) + 1).astype(F32)
    for ct in range(MIX_W // LANES):
        ln = slice(ct * LANES, (ct + 1) * LANES)
        diff = _pool_diff_tile(
            lambda j: zc_s[:, POOL_HALO - j:POOL_HALO - j + t, ln],
            zc_s[:, POOL_HALO:POOL_HALO + t, ln], pos1, ct)
        d_s[:, ln] = diff.reshape(grp * t, LANES).astype(BF16)
    mix_ref[...] = jnp.dot(d_s[...], wbd_ref[...], preferred_element_type=F32) * scale_ref[...]
    nst_ref[...] = zc_s[:, POOL_HALO + t - n_state:POOL_HALO + t, :]


def _pool_sample(proj, nb, t, state, wbd_bf16, scale, pos0, grp=SAMPLE_GROUP):
    n_state = state.shape[1]
    assert nb % grp == 0 and t % SUBLANES == 0
    assert max(POOL_WINDOWS) - 1 <= n_state <= POOL_HALO - 1
    return pl.pallas_call(
        functools.partial(_pool_sample_kernel, pos0=pos0),
        grid=(nb // grp,),
        in_specs=[
            pl.BlockSpec((grp * t, MIX_W), lambda i: (i, 0)),
            pl.BlockSpec((grp, n_state, MIX_W), lambda i: (i, 0, 0)),
            pl.BlockSpec((MIX_W, MIX_W), lambda i: (0, 0)),
            pl.BlockSpec((1, MIX_W), lambda i: (0, 0)),
        ],
        out_specs=[
            pl.BlockSpec((grp * t, MIX_W), lambda i: (i, 0)),
            pl.BlockSpec((grp, n_state, MIX_W), lambda i: (i, 0, 0)),
        ],
        out_shape=[
            jax.ShapeDtypeStruct((nb * t, MIX_W), F32),
            jax.ShapeDtypeStruct((nb, n_state, MIX_W), F32),
        ],
        scratch_shapes=[
            pltpu.VMEM((grp, POOL_HALO + t, MIX_W), F32),
            pltpu.VMEM((grp * t, MIX_W), BF16),
        ],
        compiler_params=pltpu.CompilerParams(
            dimension_semantics=("parallel",), vmem_limit_bytes=VMEM_LIMIT),
        name="pool_sample",
    )(proj, state, wbd_bf16, scale.reshape(1, MIX_W))


def _mem_out_kernel(x_ref, mix_ref, qm_ref, mk_ref, mv_ref, wmix_ref, wmem_ref, o_ref):
    grp = mk_ref.shape[0]
    rows = x_ref.shape[0]
    tq = rows // grp
    acc = x_ref[...] + jnp.dot(mix_ref[...].astype(BF16), wmix_ref[...],
                               preferred_element_type=F32)
    for h in range(MEM_HEADS):
        cols = slice(h * MEM_HD, (h + 1) * MEM_HD)
        q = qm_ref[:, cols].astype(BF16).reshape(grp, tq, MEM_HD)
        k = mk_ref[:, :, cols].astype(BF16)
        v = mv_ref[:, :, cols].astype(BF16)
        s = jnp.einsum('bqd,bkd->bqk', q, k, preferred_element_type=F32) * (MEM_HD ** -0.5)
        e = jnp.exp(s - jnp.max(s, axis=-1, keepdims=True))
        p = e / jnp.sum(e, axis=-1, keepdims=True)
        o = jnp.einsum('bqk,bkd->bqd', p.astype(BF16), v, preferred_element_type=F32)
        acc = acc + jnp.dot(o.reshape(rows, MEM_HD).astype(BF16), wmem_ref[cols, :],
                            preferred_element_type=F32)
    o_ref[...] = acc


def _mem_out(x, mix, proj, qm_col_blk, mem_k, mem_v, wmix_bf16, wmem_bf16, grp, tq):
    rows, d = x.shape
    kmix = mix.shape[1]
    n_seq, n_mem, _ = mem_k.shape
    step = grp * tq
    per_seq = rows // n_seq
    assert rows % step == 0 and tq % SUBLANES == 0
    assert (grp == 1 and per_seq % tq == 0) or per_seq == tq
    tiles_per_seq = per_seq // tq
    seq_blk = (lambda i: i // tiles_per_seq) if grp == 1 else (lambda i: i)
    return pl.pallas_call(
        _mem_out_kernel,
        grid=(rows // step,),
        in_specs=[
            pl.BlockSpec((step, d), lambda i: (i, 0)),
            pl.BlockSpec((step, kmix), lambda i: (i, 0)),
            pl.BlockSpec((step, MEM_W), lambda i: (i, qm_col_blk)),
            pl.BlockSpec((grp, n_mem, MEM_W), lambda i: (seq_blk(i), 0, 0)),
            pl.BlockSpec((grp, n_mem, MEM_W), lambda i: (seq_blk(i), 0, 0)),
            pl.BlockSpec((kmix, d), lambda i: (0, 0)),
            pl.BlockSpec((MEM_W, d), lambda i: (0, 0)),
        ],
        out_specs=pl.BlockSpec((step, d), lambda i: (i, 0)),
        out_shape=jax.ShapeDtypeStruct((rows, d), F32),
        compiler_params=pltpu.CompilerParams(
            dimension_semantics=("parallel",), vmem_limit_bytes=VMEM_LIMIT),
        name="mem_out",
    )(x, mix, proj, mem_k, mem_v, wmix_bf16, wmem_bf16)


def _dil_prompt_kernel(slope_ref, q_ref, kc_ref, kp_ref, vc_ref, vp_ref, o_ref, og_s, lse_s):
    win = pl.program_id(1)
    rows = q_ref.shape[0]
    blk = DIL_BLOCK
    slope = slope_ref[...]
    ri = lax.broadcasted_iota(jnp.int32, (blk, blk), 0)
    ci = lax.broadcasted_iota(jnp.int32, (blk, blk), 1)
    diff = (ri - ci).astype(F32)
    own_ok = ri >= ci
    prev_ok = ci >= ri
    not_first = win > 0
    scale = B_HD ** -0.5
    nt = (((1,), (1,)), ((), ()))

    for g, (w, d) in enumerate(DIL_PATTERNS):
        assert w // d == blk and rows % (blk * d) == 0
        bias_own = -(slope * float(d)) * diff
        bias_prev = -(slope * float(d)) * (diff + float(blk))
        for r in range(d):
            for j in range(rows // (blk * d)):
                own = pl.ds(j * blk * d + r, blk, stride=d) if d > 1 else pl.ds(j * blk, blk)
                if j > 0:
                    prv = (pl.ds((j - 1) * blk * d + r, blk, stride=d) if d > 1
                           else pl.ds((j - 1) * blk, blk))
                    k_prev, v_prev = kc_ref[prv, :], vc_ref[prv, :]
                else:
                    prv = (pl.ds(rows - blk * d + r, blk, stride=d) if d > 1
                           else pl.ds(rows - blk, blk))
                    k_prev, v_prev = kp_ref[prv, :], vp_ref[prv, :]
                q = q_ref[own, :].astype(BF16)
                s_own = lax.dot_general(q, kc_ref[own, :].astype(BF16), nt,
                                        preferred_element_type=F32) * scale + bias_own
                s_own = jnp.where(own_ok, s_own, NEG)
                s_prev = lax.dot_general(q, k_prev.astype(BF16), nt,
                                         preferred_element_type=F32) * scale + bias_prev
                ok = prev_ok if j > 0 else jnp.logical_and(prev_ok, not_first)
                s_prev = jnp.where(ok, s_prev, NEG)
                m = jnp.maximum(jnp.max(s_own, axis=-1, keepdims=True),
                                jnp.max(s_prev, axis=-1, keepdims=True))
                e_own = jnp.exp(s_own - m)
                e_prev = jnp.exp(s_prev - m)
                den = jnp.sum(e_own, axis=-1, keepdims=True) + jnp.sum(e_prev, axis=-1, keepdims=True)
                o = (jnp.dot((e_own / den).astype(BF16), vc_ref[own, :].astype(BF16),
                             preferred_element_type=F32)
                     + jnp.dot((e_prev / den).astype(BF16), v_prev.astype(BF16),
                               preferred_element_type=F32))
                og_s[g, own, :] = o
                lse_s[g, own, :] = jnp.broadcast_to(m + jnp.log(den), (blk, LANES))

    top = jnp.maximum(jnp.maximum(lse_s[0], lse_s[1]), lse_s[2])
    wts = [jnp.exp(lse_s[g] - top) for g in range(len(DIL_PATTERNS))]
    tot = wts[0] + wts[1] + wts[2]
    o_ref[...] = ((wts[0] / tot) * og_s[0] + (wts[1] / tot) * og_s[1] + (wts[2] / tot) * og_s[2])


def _dil_prompt(proj_pad, kv_pad, nb, seq, slopes_b):
    win = DIL_WIN
    nw = seq // win
    assert seq % win == 0 and len(DIL_PATTERNS) == 3
    cur = lambda col0: (lambda n, w, h: (n * nw + w, col0 + h))
    prev = lambda col0: (lambda n, w, h: (n * nw + jnp.maximum(w - 1, 0), col0 + h))
    return pl.pallas_call(
        _dil_prompt_kernel,
        grid=(nb, nw, B_HEADS),
        in_specs=[
            pl.BlockSpec((None, 1, LANES), lambda n, w, h: (h, 0, 0)),
            pl.BlockSpec((win, LANES), cur(0)),
            pl.BlockSpec((win, LANES), cur(0)),
            pl.BlockSpec((win, LANES), prev(0)),
            pl.BlockSpec((win, LANES), cur(B_HEADS)),
            pl.BlockSpec((win, LANES), prev(B_HEADS)),
        ],
        out_specs=pl.BlockSpec((win, LANES), cur(0)),
        out_shape=jax.ShapeDtypeStruct((nb * seq, B_HEADS * LANES), F32),
        scratch_shapes=[
            pltpu.VMEM((len(DIL_PATTERNS), win, LANES), F32),
            pltpu.VMEM((len(DIL_PATTERNS), win, LANES), F32),
        ],
        compiler_params=pltpu.CompilerParams(
            dimension_semantics=("parallel", "parallel", "parallel"),
            vmem_limit_bytes=VMEM_LIMIT),
        name="dil_prompt",
    )(slopes_b, proj_pad, kv_pad, kv_pad, kv_pad, kv_pad)


def _dil_sample_kernel(q_ref, kc_ref, vc_ref, kn_ref, vn_ref, bc_ref, bn_ref, hm_ref, o_ref,
                       knp_s, vnp_s, pc_s, *, row_lo):
    t = q_ref.shape[0]
    nt = (((1,), (1,)), ((), ()))
    scale = B_HD ** -0.5
    hm = hm_ref[...]
    qbd = (jnp.concatenate([q_ref[...]] * B_HEADS, axis=0) * hm).astype(BF16)
    knp_s[...] = jnp.zeros_like(knp_s)
    vnp_s[...] = jnp.zeros_like(vnp_s)
    knp_s[0:t, :] = kn_ref[...]
    vnp_s[0:t, :] = vn_ref[...]
    s_c = lax.dot_general(qbd, kc_ref[...].astype(BF16), nt, preferred_element_type=F32) * scale
    s_n = lax.dot_general(qbd, knp_s[...].astype(BF16), nt, preferred_element_type=F32) * scale
    stats = []
    for g in range(len(DIL_PATTERNS)):
        lo = row_lo[g]
        bc = bc_ref[g, :, lo:]
        bn = bn_ref[g]
        sc = jnp.where(bc > 0.5 * NEG, s_c[:, lo:] + bc, NEG)
        sn = jnp.where(bn > 0.5 * NEG, s_n + bn, NEG)
        m = jnp.maximum(jnp.max(sc, axis=-1, keepdims=True), jnp.max(sn, axis=-1, keepdims=True))
        ec = jnp.exp(sc - m)
        en = jnp.exp(sn - m)
        den = jnp.sum(ec, axis=-1, keepdims=True) + jnp.sum(en, axis=-1, keepdims=True)
        stats.append((ec / den, en / den, m + jnp.log(den)))
    top = jnp.maximum(jnp.maximum(stats[0][2], stats[1][2]), stats[2][2])
    wts = [jnp.exp(st[2] - top) for st in stats]
    tot = wts[0] + wts[1] + wts[2]
    pc_s[...] = jnp.zeros_like(pc_s)
    pn = jnp.zeros((B_HEADS * t, knp_s.shape[0]), F32)
    for g in range(len(DIL_PATTERNS)):
        lo = row_lo[g]
        alpha = wts[g] / tot
        pc_s[:, lo:] += alpha * stats[g][0]
        pn = pn + alpha * stats[g][1]
    o = (jnp.dot(pc_s[...].astype(BF16), vc_ref[...].astype(BF16), preferred_element_type=F32)
         + jnp.dot(pn.astype(BF16), vnp_s[...].astype(BF16), preferred_element_type=F32))
    o_ref[...] = jnp.sum((o * hm).reshape(B_HEADS, t, MIX_W), axis=0)


def _dil_sample_tables(past, t):
    slopes = np.asarray([2.0 ** (-8.0 * (h + 1) / B_HEADS) for h in range(B_HEADS)], np.float32)
    qpos = past + np.arange(t)
    col = np.arange(past + LANES)
    exists = col < past + t
    delta = qpos[:, None] - col[None, :]
    bias = np.full((len(DIL_PATTERNS), B_HEADS, t, past + LANES), NEG, np.float32)
    col_lo = []
    for g, (w, d) in enumerate(DIL_PATTERNS):
        ok = (delta >= 0) & (delta <= w) & (delta % d == 0) & exists[None, :]
        vals = -slopes[:, None, None] * delta[None].astype(np.float32)
        bias[g] = np.where(ok[None], vals, np.float32(NEG))
        col_lo.append(max(0, (past - w) // LANES * LANES))
    bias = bias.reshape(len(DIL_PATTERNS), B_HEADS * t, past + LANES)
    hm = (np.arange(MIX_W)[None, :] // B_HD == np.arange(B_HEADS * t)[:, None] // t)
    return bias[:, :, :past], bias[:, :, past:], hm.astype(np.float32), tuple(col_lo)


def _dil_sample(proj, kv_new, cache_k, cache_v):
    nb, past, _ = cache_k.shape
    t = proj.shape[0] // nb
    assert t % SUBLANES == 0 and t <= LANES and past % LANES == 0 and len(DIL_PATTERNS) == 3
    bias_c, bias_n, hm, col_lo = _dil_sample_tables(past, t)
    ng = len(DIL_PATTERNS)
    return pl.pallas_call(
        functools.partial(_dil_sample_kernel, row_lo=col_lo),
        grid=(nb,),
        in_specs=[
            pl.BlockSpec((t, MIX_W), lambda b: (b, 0)),
            pl.BlockSpec((None, past, MIX_W), lambda b: (b, 0, 0)),
            pl.BlockSpec((None, past, MIX_W), lambda b: (b, 0, 0)),
            pl.BlockSpec((t, MIX_W), lambda b: (b, 0)),
            pl.BlockSpec((t, MIX_W), lambda b: (b, 1)),
            pl.BlockSpec((ng, B_HEADS * t, past), lambda b: (0, 0, 0)),
            pl.BlockSpec((ng, B_HEADS * t, LANES), lambda b: (0, 0, 0)),
            pl.BlockSpec((B_HEADS * t, MIX_W), lambda b: (0, 0)),
        ],
        out_specs=pl.BlockSpec((t, MIX_W), lambda b: (b, 0)),
        out_shape=jax.ShapeDtypeStruct((nb * t, MIX_W), F32),
        scratch_shapes=[
            pltpu.VMEM((LANES, MIX_W), F32),
            pltpu.VMEM((LANES, MIX_W), F32),
            pltpu.VMEM((B_HEADS * t, past), F32),
        ],
        compiler_params=pltpu.CompilerParams(
            dimension_semantics=("parallel",), vmem_limit_bytes=VMEM_LIMIT),
        name="dil_sample",
    )(proj, cache_k, cache_v, kv_new, kv_new, jnp.asarray(bias_c), jnp.asarray(bias_n),
      jnp.asarray(hm))


def _pad_head_cols(w):
    d = w.shape[0]
    w = jnp.pad(w.reshape(d, B_HEADS, B_HD), ((0, 0), (0, 0), (0, LANES - B_HD)))
    return w.reshape(d, B_HEADS * LANES)


def _unpad_heads(a):
    return a.reshape(a.shape[:-1] + (B_HEADS, LANES))[..., :B_HD]


def kernel(x_prompt, x_sample, state_pool, cache_win_k, cache_win_v, cache_mem_k, cache_mem_v,
           mem_prompt, norm_mix, w_in, pool_w, pool_scale, norm_mem, w_mem_kv, w_out,
           norm_kv, w_kv, norm_ffn, peer_wq, peer_subkeys, peer_u, peer_v, norm_final):
    nb_p, seq, d = x_prompt.shape
    nb_s, dec_seq, _ = x_sample.shape
    n_mem = mem_prompt.shape[1]
    past = cache_win_k.shape[1]
    tp = nb_p * seq
    ts = nb_s * dec_seq

    mem_flat = mem_prompt.reshape(nb_p * n_mem, d)
    mkv = [_norm_matmul(mem_flat, norm_mem[l], w_mem_kv[l].astype(BF16)) for l in range(DEPTH)]
    mem_k_p = jnp.stack([m[:, :MEM_W].reshape(nb_p, n_mem, MEM_W) for m in mkv], axis=0)
    mem_v_p = jnp.stack([m[:, MEM_W:].reshape(nb_p, n_mem, MEM_W) for m in mkv], axis=0)
    mem_k_s = cache_mem_k.reshape(DEPTH, nb_s, n_mem, MEM_W)
    mem_v_s = cache_mem_v.reshape(DEPTH, nb_s, n_mem, MEM_W)

    slopes_b = jnp.broadcast_to(
        jnp.asarray([2.0 ** (-8.0 * (h + 1) / B_HEADS) for h in range(B_HEADS)], F32)[:, None, None],
        (B_HEADS, 1, LANES))
    cache_k = cache_win_k.reshape(nb_s, past, MIX_W)
    cache_v = cache_win_v.reshape(nb_s, past, MIX_W)

    xp = x_prompt.reshape(tp, d)
    xs = x_sample.reshape(ts, d)
    pool_p, pool_s = [], []
    kv_p = kv_s = None
    for l in range(DEPTH):
        w_in_l = w_in[l].astype(BF16)
        w_out_l = w_out[l].astype(BF16)
        proj_s = _norm_matmul(xs, norm_mix[l], w_in_l)
        if l < N_A:
            proj_p = _norm_matmul(xp, norm_mix[l], w_in_l)
            wbd = jax.scipy.linalg.block_diag(*[pool_w[l, g] for g in range(POOL_GROUPS)]).astype(BF16)
            mix_p, last_p = _pool_prompt(proj_p, nb_p, seq, wbd, pool_scale[l])
            pool_p.append(last_p[:, POOL_HALO - POOL_STATE:])
            mix_s, new_state = _pool_sample(proj_s, nb_s, dec_seq, state_pool[l], wbd,
                                            pool_scale[l], PAST_LEN)
            pool_s.append(new_state)
            wmix_p = w_out_l[:MIX_W]
            qm_blk_p = MIX_W // MEM_W
        else:
            w_in_pad = jnp.concatenate([_pad_head_cols(w_in[l][:, :MIX_W]), w_in[l][:, MIX_W:]],
                                       axis=1).astype(BF16)
            proj_p = _norm_matmul(xp, norm_mix[l], w_in_pad)
            mix_p = _dil_prompt(proj_p, kv_p, nb_p, seq, slopes_b)
            mix_s = _dil_sample(proj_s, kv_s, cache_k, cache_v)
            wmix_p = _pad_head_cols(w_out[l][:MIX_W].T).T.astype(BF16)
            qm_blk_p = B_HEADS * LANES // MEM_W
        xp = _mem_out(xp, mix_p, proj_p, qm_blk_p, mem_k_p[l], mem_v_p[l], wmix_p,
                      w_out_l[MIX_W:], 1, TOK_TILE)
        xs = _mem_out(xs, mix_s, proj_s, MIX_W // MEM_W, mem_k_s[l], mem_v_s[l], w_out_l[:MIX_W],
                      w_out_l[MIX_W:], SAMPLE_GROUP, dec_seq)
        peer_w = (norm_ffn[l], peer_wq[l].T.astype(BF16), peer_subkeys[l].astype(BF16),
                  peer_u[l].astype(BF16), peer_v[l].T.astype(BF16))
        xp = _peer(xp, *peer_w)
        xs = _peer(xs, *peer_w)
        if l == N_A - 1:
            w_kv_pad = jnp.concatenate([_pad_head_cols(w_kv[:, :MIX_W]),
                                        _pad_head_cols(w_kv[:, MIX_W:])], axis=1).astype(BF16)
            kv_p = _norm_matmul(xp, norm_kv, w_kv_pad)
            kv_s = _norm_matmul(xs, norm_kv, w_kv.astype(BF16))
    y_p = _final_norm(xp, norm_final)
    y_s = _final_norm(xs, norm_final)
    keep = min(WINDOW_MAX, seq)
    kv_p4 = kv_p.reshape(nb_p, seq, 2 * B_HEADS * LANES)[:, seq - keep:]
    return (y_p.reshape(nb_p, seq, d), y_s.reshape(nb_s, dec_seq, d),
            jnp.stack(pool_p, axis=0),
            _unpad_heads(kv_p4[..., :B_HEADS * LANES]), _unpad_heads(kv_p4[..., B_HEADS * LANES:]),
            mem_k_p.reshape(DEPTH, nb_p, n_mem, MEM_HEADS, MEM_HD),
            mem_v_p.reshape(DEPTH, nb_p, n_mem, MEM_HEADS, MEM_HD),
            jnp.stack(pool_s, axis=0),
            kv_s[:, :MIX_W].reshape(nb_s, dec_seq, B_HEADS, B_HD),
            kv_s[:, MIX_W:].reshape(nb_s, dec_seq, B_HEADS, B_HD))
```

```python
import functools
import math

import numpy as np
import jax
import jax.numpy as jnp
from jax import lax
from jax.experimental import pallas as pl
from jax.experimental.pallas import tpu as pltpu

F32 = jnp.float32
BF16 = jnp.bfloat16

D_MODEL = 1024
DEPTH = 4
N_A = DEPTH // 2
MIX_W = 3 * D_MODEL // 4
MEM_HEADS = 4
MEM_HD = (D_MODEL - MIX_W) // MEM_HEADS
MEM_W = MEM_HEADS * MEM_HD
POOL_WINDOWS = (2, 4, 8, 16)
POOL_GROUPS = len(POOL_WINDOWS)
POOL_GW = MIX_W // POOL_GROUPS
POOL_STATE = max(POOL_WINDOWS) - 1
B_HEADS = 8
B_HD = MIX_W // B_HEADS
DIL_PATTERNS = ((128, 1), (512, 4), (2048, 16))
WINDOW_MAX = max(w for w, _ in DIL_PATTERNS)
PEER_HEADS = 8
PEER_NKEYS = 128
PEER_EXPERTS = PEER_NKEYS * PEER_NKEYS
PEER_DK = 256
PEER_TOPK = 16
PAST_LEN = 2048
EPS = 1e-6
NEG = -1e30

LANES = 128
SUBLANES = 8
TOK_TILE = 512
PEER_TOK = 512
PEER_CHUNK_KEYS = 16
PEER_CHUNK = PEER_CHUNK_KEYS * PEER_NKEYS
PEER_PARTS = 4
PEER_PART = PEER_CHUNK // PEER_PARTS
PEER_STAIR = tuple(PEER_TOPK // (k + 1) for k in range(PEER_TOPK))
PEER_NCAND = -(-sum(PEER_STAIR) // SUBLANES) * SUBLANES
POOL_HALO = 16
DIL_BLOCK = 128
DIL_WIN = 2048
SAMPLE_GROUP = 16
VMEM_LIMIT = 56 * 1024 * 1024


def _rms(x, g):
    r = lax.rsqrt(jnp.mean(x * x, axis=-1, keepdims=True) + EPS)
    return (x * r) * g


def _norm_matmul_kernel(x_ref, g_ref, w_ref, o_ref):
    h = _rms(x_ref[...], g_ref[...])
    o_ref[...] = jnp.dot(h.astype(BF16), w_ref[...], preferred_element_type=F32)


def _norm_matmul(x, g, w_bf16, tile=TOK_TILE):
    t, d = x.shape
    n = w_bf16.shape[1]
    tile = min(tile, t)
    assert t % tile == 0
    return pl.pallas_call(
        _norm_matmul_kernel,
        grid=(t // tile,),
        in_specs=[
            pl.BlockSpec((tile, d), lambda i: (i, 0)),
            pl.BlockSpec((1, d), lambda i: (0, 0)),
            pl.BlockSpec((d, n), lambda i: (0, 0)),
        ],
        out_specs=pl.BlockSpec((tile, n), lambda i: (i, 0)),
        out_shape=jax.ShapeDtypeStruct((t, n), F32),
        compiler_params=pltpu.CompilerParams(
            dimension_semantics=("parallel",), vmem_limit_bytes=VMEM_LIMIT),
        name="norm_matmul",
    )(x, g.reshape(1, d), w_bf16)


def _final_norm_kernel(x_ref, g_ref, o_ref):
    o_ref[...] = _rms(x_ref[...], g_ref[...])


def _final_norm(x, g, tile=TOK_TILE):
    t, d = x.shape
    assert t % tile == 0
    return pl.pallas_call(
        _final_norm_kernel,
        grid=(t // tile,),
        in_specs=[
            pl.BlockSpec((tile, d), lambda i: (i, 0)),
            pl.BlockSpec((1, d), lambda i: (0, 0)),
        ],
        out_specs=pl.BlockSpec((tile, d), lambda i: (i, 0)),
        out_shape=jax.ShapeDtypeStruct((t, d), F32),
        compiler_params=pltpu.CompilerParams(dimension_semantics=("parallel",)),
        name="final_norm",
    )(x, g.reshape(1, d))


def _extract_top(s, out_ref, n):
    for k in range(n):
        m = jnp.max(s, axis=0, keepdims=True)
        out_ref[k:k + 1, :] = m
        s = jnp.where(s == m, -jnp.inf, s)


def _peer_kernel(x_ref, g_ref, wqt_ref, sk_ref, u_ref, vt_ref, o_ref,
                 ht_s, q_s, s2_s, e2_s, s1r_s, e1r_s, tau_s, a_s, b_s, cand_s, top_s,
                 sc_s, w_s, acc_s):
    c = pl.program_id(1)
    tok = x_ref.shape[0]
    n_lt = tok // LANES

    @pl.when(c == 0)
    def _route():
        h = _rms(x_ref[...], g_ref[...])
        ht_s[...] = h.T.astype(BF16)
        q_s[...] = jnp.dot(wqt_ref[...], ht_s[...],
                           preferred_element_type=F32).astype(BF16)
        acc_s[...] = jnp.zeros_like(acc_s)
        cand_s[...] = jnp.full(cand_s.shape, -jnp.inf, F32)

        def head_body(hd, carry):
            base = pl.multiple_of(hd * PEER_DK, PEER_DK)
            half = PEER_DK // 2
            row0 = pl.multiple_of(hd * PEER_NKEYS, PEER_NKEYS)
            s1 = jnp.dot(sk_ref[0], q_s[pl.ds(base, half), :], preferred_element_type=F32)
            s2 = jnp.dot(sk_ref[1], q_s[pl.ds(base + half, half), :],
                         preferred_element_type=F32)
            s2_s[hd] = s2
            sc_s[pl.ds(row0, PEER_NKEYS), :] = s1
            for lt in range(n_lt):
                ln = slice(lt * LANES, (lt + 1) * LANES)
                _extract_top(s1[:, ln], a_s, PEER_TOPK)
                _extract_top(s2[:, ln], b_s, PEER_TOPK)
                off = 0
                for k, width in enumerate(PEER_STAIR):
                    cand_s[off:off + width, :] = a_s[k:k + 1, :] + b_s[0:width, :]
                    off += width
                _extract_top(cand_s[...], top_s, PEER_TOPK)
                top = top_s[...]
                m0 = top[0:1, :]
                den = jnp.sum(jnp.exp(top - m0), axis=0, keepdims=True)
                tau_s[hd, :, ln] = top[PEER_TOPK - 1:PEER_TOPK, :]
                sc_s[pl.ds(PEER_HEADS * PEER_NKEYS + row0, PEER_NKEYS), ln] = (
                    jnp.exp(s1[:, ln] - a_s[0:1, :]) / den)
                e2_s[hd, :, ln] = jnp.exp(s2[:, ln] - b_s[0:1, :])
            return carry

        lax.fori_loop(0, PEER_HEADS, head_body, 0)

        for hd in range(PEER_HEADS):
            rows = pl.ds(hd, PEER_NKEYS, stride=PEER_HEADS)
            for lt in range(n_lt):
                ln = slice(lt * LANES, (lt + 1) * LANES)
                s1r_s[lt, rows, :] = sc_s[hd * PEER_NKEYS:(hd + 1) * PEER_NKEYS, ln]
                e1r_s[lt, rows, :] = sc_s[(PEER_HEADS + hd) * PEER_NKEYS:
                                          (PEER_HEADS + hd + 1) * PEER_NKEYS, ln]

    def pre_dot(part):
        rows = slice(part * PEER_PART, (part + 1) * PEER_PART)
        sc_s[rows, :] = jnp.dot(u_ref[rows, :], ht_s[...], preferred_element_type=F32)

    def build(part):
        for k in range(PEER_PART // PEER_NKEYS):
            j = part * (PEER_PART // PEER_NKEYS) + k
            i1 = c * PEER_CHUNK_KEYS + j
            r0 = pl.multiple_of(i1 * PEER_HEADS, PEER_HEADS)
            rows = slice(j * PEER_NKEYS, (j + 1) * PEER_NKEYS)
            for lt in range(n_lt):
                ln = slice(lt * LANES, (lt + 1) * LANES)
                wgt = jnp.zeros((PEER_NKEYS, LANES), F32)
                s1_rows = s1r_s[lt, pl.ds(r0, PEER_HEADS), :]
                e1_rows = e1r_s[lt, pl.ds(r0, PEER_HEADS), :]
                for hd in range(PEER_HEADS):
                    t = s1_rows[hd:hd + 1, :] + s2_s[hd, :, ln]
                    p = e1_rows[hd:hd + 1, :] * e2_s[hd, :, ln]
                    wgt = wgt + jnp.where(t >= tau_s[hd, :, ln], p, 0.0)
                pre = sc_s[rows, ln]
                act = 0.5 * pre * (1.0 + lax.erf(pre * (1.0 / math.sqrt(2.0))))
                w_s[rows, ln] = (wgt * act).astype(BF16)

    def out_dot(part):
        rows = slice(part * PEER_PART, (part + 1) * PEER_PART)
        acc_s[...] += jnp.dot(vt_ref[:, rows], w_s[rows, :], preferred_element_type=F32)

    pre_dot(0)
    for part in range(PEER_PARTS):
        if part + 1 < PEER_PARTS:
            pre_dot(part + 1)
        build(part)
        if part > 0:
            out_dot(part - 1)
    out_dot(PEER_PARTS - 1)

    @pl.when(c == pl.num_programs(1) - 1)
    def _finish():
        o_ref[...] = x_ref[...] + acc_s[...].T


def _peer(x, g, wqt_bf16, sk_bf16, u_bf16, vt_bf16, tok=PEER_TOK):
    t, d = x.shape
    assert t % tok == 0 and 2 * PEER_HEADS * PEER_NKEYS <= PEER_CHUNK
    n_chunks = PEER_EXPERTS // PEER_CHUNK
    hq = PEER_HEADS * PEER_DK
    return pl.pallas_call(
        _peer_kernel,
        grid=(t // tok, n_chunks),
        in_specs=[
            pl.BlockSpec((tok, d), lambda i, c: (i, 0)),
            pl.BlockSpec((1, d), lambda i, c: (0, 0)),
            pl.BlockSpec((hq, d), lambda i, c: (0, 0), pipeline_mode=pl.Buffered(1)),
            pl.BlockSpec((2, PEER_NKEYS, PEER_DK // 2), lambda i, c: (0, 0, 0)),
            pl.BlockSpec((PEER_CHUNK, d), lambda i, c: (c, 0)),
            pl.BlockSpec((d, PEER_CHUNK), lambda i, c: (0, c)),
        ],
        out_specs=pl.BlockSpec((tok, d), lambda i, c: (i, 0)),
        out_shape=jax.ShapeDtypeStruct((t, d), F32),
        scratch_shapes=[
            pltpu.VMEM((d, tok), BF16),
            pltpu.VMEM((hq, tok), BF16),
            pltpu.VMEM((PEER_HEADS, PEER_NKEYS, tok), F32),
            pltpu.VMEM((PEER_HEADS, PEER_NKEYS, tok), F32),
            pltpu.VMEM((tok // LANES, PEER_NKEYS * PEER_HEADS, LANES), F32),
            pltpu.VMEM((tok // LANES, PEER_NKEYS * PEER_HEADS, LANES), F32),
            pltpu.VMEM((PEER_HEADS, 1, tok), F32),
            pltpu.VMEM((PEER_TOPK, LANES), F32),
            pltpu.VMEM((PEER_TOPK, LANES), F32),
            pltpu.VMEM((PEER_NCAND, LANES), F32),
            pltpu.VMEM((PEER_TOPK, LANES), F32),
            pltpu.VMEM((PEER_CHUNK, tok), F32),
            pltpu.VMEM((PEER_CHUNK, tok), BF16),
            pltpu.VMEM((d, tok), F32),
        ],
        compiler_params=pltpu.CompilerParams(
            dimension_semantics=("parallel", "arbitrary"), vmem_limit_bytes=VMEM_LIMIT),
        name="peer",
    )(x, g.reshape(1, d), wqt_bf16, sk_bf16, u_bf16, vt_bf16)


def _pool_tile_plan(ct):
    lo = ct * LANES
    hi = lo + LANES - 1
    return [(POOL_WINDOWS[g], (g + 1) * POOL_GW) for g in range(lo // POOL_GW, hi // POOL_GW + 1)]


def _pool_diff_tile(load_shifted, z_tile, pos1, ct):
    plan = _pool_tile_plan(ct)
    wanted = {w for w, _ in plan}
    acc = z_tile
    snaps = {}
    for j in range(1, max(wanted)):
        acc = acc + load_shifted(j)
        if j + 1 in wanted:
            snaps[j + 1] = acc
    if len(plan) == 1:
        w = plan[0][0]
        win = snaps[w]
        cnt = jnp.minimum(float(w), pos1)
    else:
        (w_lo, edge), (w_hi, _) = plan
        lane = lax.broadcasted_iota(jnp.int32, (1,) * (z_tile.ndim - 1) + (LANES,), z_tile.ndim - 1)
        in_lo = lane + ct * LANES < edge
        win = jnp.where(in_lo, snaps[w_lo], snaps[w_hi])
        cnt = jnp.minimum(jnp.where(in_lo, float(w_lo), float(w_hi)), pos1)
    return win / cnt - z_tile


def _pool_prompt_kernel(z_ref, wbd_ref, scale_ref, mix_ref, state_ref, zc_s, d_s):
    i = pl.program_id(1)
    tile = z_ref.shape[0]

    @pl.when(i == 0)
    def _start():
        zc_s[0:POOL_HALO, :] = jnp.zeros((POOL_HALO, MIX_W), F32)

    zc_s[POOL_HALO:POOL_HALO + tile, :] = z_ref[...]
    pos1 = (i * tile + lax.broadcasted_iota(jnp.int32, (tile, 1), 0) + 1).astype(F32)
    for ct in range(MIX_W // LANES):
        ln = slice(ct * LANES, (ct + 1) * LANES)
        diff = _pool_diff_tile(
            lambda j: zc_s[POOL_HALO - j:POOL_HALO - j + tile, ln], z_ref[:, ln], pos1, ct)
        d_s[:, ln] = diff.astype(BF16)
    mix_ref[...] = jnp.dot(d_s[...], wbd_ref[...], preferred_element_type=F32) * scale_ref[...]
    last = z_ref[tile - POOL_HALO:tile, :]
    state_ref[...] = last
    zc_s[0:POOL_HALO, :] = last


def _pool_prompt(proj, nb, seq, wbd_bf16, scale, tile=TOK_TILE):
    nt = seq // tile
    assert seq % tile == 0 and tile >= POOL_HALO
    return pl.pallas_call(
        _pool_prompt_kernel,
        grid=(nb, nt),
        in_specs=[
            pl.BlockSpec((tile, MIX_W), lambda n, i: (n * nt + i, 0)),
            pl.BlockSpec((MIX_W, MIX_W), lambda n, i: (0, 0)),
            pl.BlockSpec((1, MIX_W), lambda n, i: (0, 0)),
        ],
        out_specs=[
            pl.BlockSpec((tile, MIX_W), lambda n, i: (n * nt + i, 0)),
            pl.BlockSpec((None, POOL_HALO, MIX_W), lambda n, i: (n, 0, 0)),
        ],
        out_shape=[
            jax.ShapeDtypeStruct((nb * seq, MIX_W), F32),
            jax.ShapeDtypeStruct((nb, POOL_HALO, MIX_W), F32),
        ],
        scratch_shapes=[
            pltpu.VMEM((POOL_HALO + tile, MIX_W), F32),
            pltpu.VMEM((tile, MIX_W), BF16),
        ],
        compiler_params=pltpu.CompilerParams(
            dimension_semantics=("arbitrary", "arbitrary"), vmem_limit_bytes=VMEM_LIMIT),
        name="pool_prompt",
    )(proj, wbd_bf16, scale.reshape(1, MIX_W))


def _pool_sample_kernel(z_ref, st_ref, wbd_ref, scale_ref, mix_ref, nst_ref, zc_s, d_s, *, pos0):
    grp, n_state, _ = st_ref.shape
    t = z_ref.shape[0] // grp
    zc_s[:, POOL_HALO - n_state:POOL_HALO, :] = st_ref[...]
    zc_s[:, POOL_HALO:POOL_HALO + t, :] = z_ref[...].reshape(grp, t, MIX_W)
    pos1 = (pos0 + lax.broadcasted_iota(jnp.int32, (1, t, 1), 1) + 1).astype(F32)
    for ct in range(MIX_W // LANES):
        ln = slice(ct * LANES, (ct + 1) * LANES)
        diff = _pool_diff_tile(
            lambda j: zc_s[:, POOL_HALO - j:POOL_HALO - j + t, ln],
            zc_s[:, POOL_HALO:POOL_HALO + t, ln], pos1, ct)
        d_s[:, ln] = diff.reshape(grp * t, LANES).astype(BF16)
    mix_ref[...] = jnp.dot(d_s[...], wbd_ref[...], preferred_element_type=F32) * scale_ref[...]
    nst_ref[...] = zc_s[:, POOL_HALO + t - n_state:POOL_HALO + t, :]


def _pool_sample(proj, nb, t, state, wbd_bf16, scale, pos0, grp=SAMPLE_GROUP):
    n_state = state.shape[1]
    assert nb % grp == 0 and t % SUBLANES == 0
    assert max(POOL_WINDOWS) - 1 <= n_state <= POOL_HALO - 1
    return pl.pallas_call(
        functools.partial(_pool_sample_kernel, pos0=pos0),
        grid=(nb // grp,),
        in_specs=[
            pl.BlockSpec((grp * t, MIX_W), lambda i: (i, 0)),
            pl.BlockSpec((grp, n_state, MIX_W), lambda i: (i, 0, 0)),
            pl.BlockSpec((MIX_W, MIX_W), lambda i: (0, 0)),
            pl.BlockSpec((1, MIX_W), lambda i: (0, 0)),
        ],
        out_specs=[
            pl.BlockSpec((grp * t, MIX_W), lambda i: (i, 0)),
            pl.BlockSpec((grp, n_state, MIX_W), lambda i: (i, 0, 0)),
        ],
        out_shape=[
            jax.ShapeDtypeStruct((nb * t, MIX_W), F32),
            jax.ShapeDtypeStruct((nb, n_state, MIX_W), F32),
        ],
        scratch_shapes=[
            pltpu.VMEM((grp, POOL_HALO + t, MIX_W), F32),
            pltpu.VMEM((grp * t, MIX_W), BF16),
        ],
        compiler_params=pltpu.CompilerParams(
            dimension_semantics=("parallel",), vmem_limit_bytes=VMEM_LIMIT),
        name="pool_sample",
    )(proj, state, wbd_bf16, scale.reshape(1, MIX_W))


def _mem_out_kernel(x_ref, mix_ref, qm_ref, mk_ref, mv_ref, wmix_ref, wmem_ref, o_ref):
    grp = mk_ref.shape[0]
    rows = x_ref.shape[0]
    tq = rows // grp
    acc = x_ref[...] + jnp.dot(mix_ref[...].astype(BF16), wmix_ref[...],
                               preferred_element_type=F32)
    for h in range(MEM_HEADS):
        cols = slice(h * MEM_HD, (h + 1) * MEM_HD)
        q = qm_ref[:, cols].astype(BF16).reshape(grp, tq, MEM_HD)
        k = mk_ref[:, :, cols].astype(BF16)
        v = mv_ref[:, :, cols].astype(BF16)
        s = jnp.einsum('bqd,bkd->bqk', q, k, preferred_element_type=F32) * (MEM_HD ** -0.5)
        e = jnp.exp(s - jnp.max(s, axis=-1, keepdims=True))
        p = e / jnp.sum(e, axis=-1, keepdims=True)
        o = jnp.einsum('bqk,bkd->bqd', p.astype(BF16), v, preferred_element_type=F32)
        acc = acc + jnp.dot(o.reshape(rows, MEM_HD).astype(BF16), wmem_ref[cols, :],
                            preferred_element_type=F32)
    o_ref[...] = acc


def _mem_out(x, mix, proj, qm_col_blk, mem_k, mem_v, wmix_bf16, wmem_bf16, grp, tq):
    rows, d = x.shape
    kmix = mix.shape[1]
    n_seq, n_mem, _ = mem_k.shape
    step = grp * tq
    per_seq = rows // n_seq
    assert rows % step == 0 and tq % SUBLANES == 0
    assert (grp == 1 and per_seq % tq == 0) or per_seq == tq
    tiles_per_seq = per_seq // tq
    seq_blk = (lambda i: i // tiles_per_seq) if grp == 1 else (lambda i: i)
    return pl.pallas_call(
        _mem_out_kernel,
        grid=(rows // step,),
        in_specs=[
            pl.BlockSpec((step, d), lambda i: (i, 0)),
            pl.BlockSpec((step, kmix), lambda i: (i, 0)),
            pl.BlockSpec((step, MEM_W), lambda i: (i, qm_col_blk)),
            pl.BlockSpec((grp, n_mem, MEM_W), lambda i: (seq_blk(i), 0, 0)),
            pl.BlockSpec((grp, n_mem, MEM_W), lambda i: (seq_blk(i), 0, 0)),
            pl.BlockSpec((kmix, d), lambda i: (0, 0)),
            pl.BlockSpec((MEM_W, d), lambda i: (0, 0)),
        ],
        out_specs=pl.BlockSpec((step, d), lambda i: (i, 0)),
        out_shape=jax.ShapeDtypeStruct((rows, d), F32),
        compiler_params=pltpu.CompilerParams(
            dimension_semantics=("parallel",), vmem_limit_bytes=VMEM_LIMIT),
        name="mem_out",
    )(x, mix, proj, mem_k, mem_v, wmix_bf16, wmem_bf16)


def _dil_prompt_kernel(slope_ref, q_ref, kc_ref, kp_ref, vc_ref, vp_ref, o_ref, og_s, lse_s):
    win = pl.program_id(1)
    rows = q_ref.shape[0]
    blk = DIL_BLOCK
    slope = slope_ref[...]
    ri = lax.broadcasted_iota(jnp.int32, (blk, blk), 0)
    ci = lax.broadcasted_iota(jnp.int32, (blk, blk), 1)
    diff = (ri - ci).astype(F32)
    own_ok = ri >= ci
    prev_ok = ci >= ri
    not_first = win > 0
    scale = B_HD ** -0.5
    nt = (((1,), (1,)), ((), ()))

    for g, (w, d) in enumerate(DIL_PATTERNS):
        assert w // d == blk and rows % (blk * d) == 0
        bias_own = -(slope * float(d)) * diff
        bias_prev = -(slope * float(d)) * (diff + float(blk))
        for r in range(d):
            for j in range(rows // (blk * d)):
                own = pl.ds(j * blk * d + r, blk, stride=d) if d > 1 else pl.ds(j * blk, blk)
                if j > 0:
                    prv = (pl.ds((j - 1) * blk * d + r, blk, stride=d) if d > 1
                           else pl.ds((j - 1) * blk, blk))
                    k_prev, v_prev = kc_ref[prv, :], vc_ref[prv, :]
                else:
                    prv = (pl.ds(rows - blk * d + r, blk, stride=d) if d > 1
                           else pl.ds(rows - blk, blk))
                    k_prev, v_prev = kp_ref[prv, :], vp_ref[prv, :]
                q = q_ref[own, :].astype(BF16)
                s_own = lax.dot_general(q, kc_ref[own, :].astype(BF16), nt,
                                        preferred_element_type=F32) * scale + bias_own
                s_own = jnp.where(own_ok, s_own, NEG)
                s_prev = lax.dot_general(q, k_prev.astype(BF16), nt,
                                         preferred_element_type=F32) * scale + bias_prev
                ok = prev_ok if j > 0 else jnp.logical_and(prev_ok, not_first)
                s_prev = jnp.where(ok, s_prev, NEG)
                m = jnp.maximum(jnp.max(s_own, axis=-1, keepdims=True),
                                jnp.max(s_prev, axis=-1, keepdims=True))
                e_own = jnp.exp(s_own - m)
                e_prev = jnp.exp(s_prev - m)
                den = jnp.sum(e_own, axis=-1, keepdims=True) + jnp.sum(e_prev, axis=-1, keepdims=True)
                o = (jnp.dot((e_own / den).astype(BF16), vc_ref[own, :].astype(BF16),
                             preferred_element_type=F32)
                     + jnp.dot((e_prev / den).astype(BF16), v_prev.astype(BF16),
                               preferred_element_type=F32))
                og_s[g, own, :] = o
                lse_s[g, own, :] = jnp.broadcast_to(m + jnp.log(den), (blk, LANES))

    top = jnp.maximum(jnp.maximum(lse_s[0], lse_s[1]), lse_s[2])
    wts = [jnp.exp(lse_s[g] - top) for g in range(len(DIL_PATTERNS))]
    tot = wts[0] + wts[1] + wts[2]
    o_ref[...] = ((wts[0] / tot) * og_s[0] + (wts[1] / tot) * og_s[1] + (wts[2] / tot) * og_s[2])


def _dil_prompt(proj_pad, kv_pad, nb, seq, slopes_b):
    win = DIL_WIN
    nw = seq // win
    assert seq % win == 0 and len(DIL_PATTERNS) == 3
    cur = lambda col0: (lambda n, w, h: (n * nw + w, col0 + h))
    prev = lambda col0: (lambda n, w, h: (n * nw + jnp.maximum(w - 1, 0), col0 + h))
    return pl.pallas_call(
        _dil_prompt_kernel,
        grid=(nb, nw, B_HEADS),
        in_specs=[
            pl.BlockSpec((None, 1, LANES), lambda n, w, h: (h, 0, 0)),
            pl.BlockSpec((win, LANES), cur(0)),
            pl.BlockSpec((win, LANES), cur(0)),
            pl.BlockSpec((win, LANES), prev(0)),
            pl.BlockSpec((win, LANES), cur(B_HEADS)),
            pl.BlockSpec((win, LANES), prev(B_HEADS)),
        ],
        out_specs=pl.BlockSpec((win, LANES), cur(0)),
        out_shape=jax.ShapeDtypeStruct((nb * seq, B_HEADS * LANES), F32),
        scratch_shapes=[
            pltpu.VMEM((len(DIL_PATTERNS), win, LANES), F32),
            pltpu.VMEM((len(DIL_PATTERNS), win, LANES), F32),
        ],
        compiler_params=pltpu.CompilerParams(
            dimension_semantics=("parallel", "parallel", "parallel"),
            vmem_limit_bytes=VMEM_LIMIT),
        name="dil_prompt",
    )(slopes_b, proj_pad, kv_pad, kv_pad, kv_pad, kv_pad)


def _dil_sample_kernel(q_ref, kc_ref, vc_ref, kn_ref, vn_ref, bc_ref, bn_ref, hm_ref, o_ref,
                       knp_s, vnp_s, pc_s, *, row_lo):
    t = q_ref.shape[0]
    nt = (((1,), (1,)), ((), ()))
    scale = B_HD ** -0.5
    hm = hm_ref[...]
    qbd = (jnp.concatenate([q_ref[...]] * B_HEADS, axis=0) * hm).astype(BF16)
    knp_s[...] = jnp.zeros_like(knp_s)
    vnp_s[...] = jnp.zeros_like(vnp_s)
    knp_s[0:t, :] = kn_ref[...]
    vnp_s[0:t, :] = vn_ref[...]
    s_c = lax.dot_general(qbd, kc_ref[...].astype(BF16), nt, preferred_element_type=F32) * scale
    s_n = lax.dot_general(qbd, knp_s[...].astype(BF16), nt, preferred_element_type=F32) * scale
    stats = []
    for g in range(len(DIL_PATTERNS)):
        lo = row_lo[g]
        bc = bc_ref[g, :, lo:]
        bn = bn_ref[g]
        sc = jnp.where(bc > 0.5 * NEG, s_c[:, lo:] + bc, NEG)
        sn = jnp.where(bn > 0.5 * NEG, s_n + bn, NEG)
        m = jnp.maximum(jnp.max(sc, axis=-1, keepdims=True), jnp.max(sn, axis=-1, keepdims=True))
        ec = jnp.exp(sc - m)
        en = jnp.exp(sn - m)
        den = jnp.sum(ec, axis=-1, keepdims=True) + jnp.sum(en, axis=-1, keepdims=True)
        stats.append((ec / den, en / den, m + jnp.log(den)))
    top = jnp.maximum(jnp.maximum(stats[0][2], stats[1][2]), stats[2][2])
    wts = [jnp.exp(st[2] - top) for st in stats]
    tot = wts[0] + wts[1] + wts[2]
    pc_s[...] = jnp.zeros_like(pc_s)
    pn = jnp.zeros((B_HEADS * t, knp_s.shape[0]), F32)
    for g in range(len(DIL_PATTERNS)):
        lo = row_lo[g]
        alpha = wts[g] / tot
        pc_s[:, lo:] += alpha * stats[g][0]
        pn = pn + alpha * stats[g][1]
    o = (jnp.dot(pc_s[...].astype(BF16), vc_ref[...].astype(BF16), preferred_element_type=F32)
         + jnp.dot(pn.astype(BF16), vnp_s[...].astype(BF16), preferred_element_type=F32))
    o_ref[...] = jnp.sum((o * hm).reshape(B_HEADS, t, MIX_W), axis=0)


def _dil_sample_tables(past, t):
    slopes = np.asarray([2.0 ** (-8.0 * (h + 1) / B_HEADS) for h in range(B_HEADS)], np.float32)
    qpos = past + np.arange(t)
    col = np.arange(past + LANES)
    exists = col < past + t
    delta = qpos[:, None] - col[None, :]
    bias = np.full((len(DIL_PATTERNS), B_HEADS, t, past + LANES), NEG, np.float32)
    col_lo = []
    for g, (w, d) in enumerate(DIL_PATTERNS):
        ok = (delta >= 0) & (delta <= w) & (delta % d == 0) & exists[None, :]
        vals = -slopes[:, None, None] * delta[None].astype(np.float32)
        bias[g] = np.where(ok[None], vals, np.float32(NEG))
        col_lo.append(max(0, (past - w) // LANES * LANES))
    bias = bias.reshape(len(DIL_PATTERNS), B_HEADS * t, past + LANES)
    hm = (np.arange(MIX_W)[None, :] // B_HD == np.arange(B_HEADS * t)[:, None] // t)
    return bias[:, :, :past], bias[:, :, past:], hm.astype(np.float32), tuple(col_lo)


def _dil_sample(proj, kv_new, cache_k, cache_v):
    nb, past, _ = cache_k.shape
    t = proj.shape[0] // nb
    assert t % SUBLANES == 0 and t <= LANES and past % LANES == 0 and len(DIL_PATTERNS) == 3
    bias_c, bias_n, hm, col_lo = _dil_sample_tables(past, t)
    ng = len(DIL_PATTERNS)
    return pl.pallas_call(
        functools.partial(_dil_sample_kernel, row_lo=col_lo),
        grid=(nb,),
        in_specs=[
            pl.BlockSpec((t, MIX_W), lambda b: (b, 0)),
            pl.BlockSpec((None, past, MIX_W), lambda b: (b, 0, 0)),
            pl.BlockSpec((None, past, MIX_W), lambda b: (b, 0, 0)),
            pl.BlockSpec((t, MIX_W), lambda b: (b, 0)),
            pl.BlockSpec((t, MIX_W), lambda b: (b, 1)),
            pl.BlockSpec((ng, B_HEADS * t, past), lambda b: (0, 0, 0)),
            pl.BlockSpec((ng, B_HEADS * t, LANES), lambda b: (0, 0, 0)),
            pl.BlockSpec((B_HEADS * t, MIX_W), lambda b: (0, 0)),
        ],
        out_specs=pl.BlockSpec((t, MIX_W), lambda b: (b, 0)),
        out_shape=jax.ShapeDtypeStruct((nb * t, MIX_W), F32),
        scratch_shapes=[
            pltpu.VMEM((LANES, MIX_W), F32),
            pltpu.VMEM((LANES, MIX_W), F32),
            pltpu.VMEM((B_HEADS * t, past), F32),
        ],
        compiler_params=pltpu.CompilerParams(
            dimension_semantics=("parallel",), vmem_limit_bytes=VMEM_LIMIT),
        name="dil_sample",
    )(proj, cache_k, cache_v, kv_new, kv_new, jnp.asarray(bias_c), jnp.asarray(bias_n),
      jnp.asarray(hm))


def _pad_head_cols(w):
    d = w.shape[0]
    w = jnp.pad(w.reshape(d, B_HEADS, B_HD), ((0, 0), (0, 0), (0, LANES - B_HD)))
    return w.reshape(d, B_HEADS * LANES)


def _unpad_heads(a):
    return a.reshape(a.shape[:-1] + (B_HEADS, LANES))[..., :B_HD]


def kernel(x_prompt, x_sample, state_pool, cache_win_k, cache_win_v, cache_mem_k, cache_mem_v,
           mem_prompt, norm_mix, w_in, pool_w, pool_scale, norm_mem, w_mem_kv, w_out,
           norm_kv, w_kv, norm_ffn, peer_wq, peer_subkeys, peer_u, peer_v, norm_final):
    nb_p, seq, d = x_prompt.shape
    nb_s, dec_seq, _ = x_sample.shape
    n_mem = mem_prompt.shape[1]
    past = cache_win_k.shape[1]
    tp = nb_p * seq
    ts = nb_s * dec_seq

    mem_flat = mem_prompt.reshape(nb_p * n_mem, d)
    mkv = [_norm_matmul(mem_flat, norm_mem[l], w_mem_kv[l].astype(BF16)) for l in range(DEPTH)]
    mem_k_p = jnp.stack([m[:, :MEM_W].reshape(nb_p, n_mem, MEM_W) for m in mkv], axis=0)
    mem_v_p = jnp.stack([m[:, MEM_W:].reshape(nb_p, n_mem, MEM_W) for m in mkv], axis=0)
    mem_k_s = cache_mem_k.reshape(DEPTH, nb_s, n_mem, MEM_W)
    mem_v_s = cache_mem_v.reshape(DEPTH, nb_s, n_mem, MEM_W)

    slopes_b = jnp.broadcast_to(
        jnp.asarray([2.0 ** (-8.0 * (h + 1) / B_HEADS) for h in range(B_HEADS)], F32)[:, None, None],
        (B_HEADS, 1, LANES))
    cache_k = cache_win_k.reshape(nb_s, past, MIX_W)
    cache_v = cache_win_v.reshape(nb_s, past, MIX_W)

    xp = x_prompt.reshape(tp, d)
    xs = x_sample.reshape(ts, d)
    pool_p, pool_s = [], []
    kv_p = kv_s = None
    for l in range(DEPTH):
        w_in_l = w_in[l].astype(BF16)
        w_out_l = w_out[l].astype(BF16)
        proj_s = _norm_matmul(xs, norm_mix[l], w_in_l)
        if l < N_A:
            proj_p = _norm_matmul(xp, norm_mix[l], w_in_l)
            wbd = jax.scipy.linalg.block_diag(*[pool_w[l, g] for g in range(POOL_GROUPS)]).astype(BF16)
            mix_p, last_p = _pool_prompt(proj_p, nb_p, seq, wbd, pool_scale[l])
            pool_p.append(last_p[:, POOL_HALO - POOL_STATE:])
            mix_s, new_state = _pool_sample(proj_s, nb_s, dec_seq, state_pool[l], wbd,
                                            pool_scale[l], PAST_LEN)
            pool_s.append(new_state)
            wmix_p = w_out_l[:MIX_W]
            qm_blk_p = MIX_W // MEM_W
        else:
            w_in_pad = jnp.concatenate([_pad_head_cols(w_in[l][:, :MIX_W]), w_in[l][:, MIX_W:]],
                                       axis=1).astype(BF16)
            proj_p = _norm_matmul(xp, norm_mix[l], w_in_pad)
            mix_p = _dil_prompt(proj_p, kv_p, nb_p, seq, slopes_b)
            mix_s = _dil_sample(proj_s, kv_s, cache_k, cache_v)
            wmix_p = _pad_head_cols(w_out[l][:MIX_W].T).T.astype(BF16)
            qm_blk_p = B_HEADS * LANES // MEM_W
        xp = _mem_out(xp, mix_p, proj_p, qm_blk_p, mem_k_p[l], mem_v_p[l], wmix_p,
                      w_out_l[MIX_W:], 1, TOK_TILE)
        xs = _mem_out(xs, mix_s, proj_s, MIX_W // MEM_W, mem_k_s[l], mem_v_s[l], w_out_l[:MIX_W],
                      w_out_l[MIX_W:], SAMPLE_GROUP, dec_seq)
        peer_w = (norm_ffn[l], peer_wq[l].T.astype(BF16), peer_subkeys[l].astype(BF16),
                  peer_u[l].astype(BF16), peer_v[l].T.astype(BF16))
        xp = _peer(xp, *peer_w)
        xs = _peer(xs, *peer_w)
        if l == N_A - 1:
            w_kv_pad = jnp.concatenate([_pad_head_cols(w_kv[:, :MIX_W]),
                                        _pad_head_cols(w_kv[:, MIX_W:])], axis=1).astype(BF16)
            kv_p = _norm_matmul(xp, norm_kv, w_kv_pad)
            kv_s = _norm_matmul(xs, norm_kv, w_kv.astype(BF16))
    y_p = _final_norm(xp, norm_final)
    y_s = _final_norm(xs, norm_final)
    keep = min(WINDOW_MAX, seq)
    kv_p4 = kv_p.reshape(nb_p, seq, 2 * B_HEADS * LANES)[:, seq - keep:]
    return (y_p.reshape(nb_p, seq, d), y_s.reshape(nb_s, dec_seq, d),
            jnp.stack(pool_p, axis=0),
            _unpad_heads(kv_p4[..., :B_HEADS * LANES]), _unpad_heads(kv_p4[..., B_HEADS * LANES:]),
            mem_k_p.reshape(DEPTH, nb_p, n_mem, MEM_HEADS, MEM_HD),
            mem_v_p.reshape(DEPTH, nb_p, n_mem, MEM_HEADS, MEM_HD),
            jnp.stack(pool_s, axis=0),
            kv_s[:, :MIX_W].reshape(nb_s, dec_seq, B_HEADS, B_HD),
            kv_s[:, MIX_W:].reshape(nb_s, dec_seq, B_HEADS, B_HD))
```

```python
import functools
import math

import numpy as np
import jax
import jax.numpy as jnp
from jax import lax
from jax.experimental import pallas as pl
from jax.experimental.pallas import tpu as pltpu

F32 = jnp.float32
BF16 = jnp.bfloat16

D_MODEL = 1024
DEPTH = 4
N_A = DEPTH // 2
MIX_W = 3 * D_MODEL // 4
MEM_HEADS = 4
MEM_HD = (D_MODEL - MIX_W) // MEM_HEADS
MEM_W = MEM_HEADS * MEM_HD
POOL_WINDOWS = (2, 4, 8, 16)
POOL_GROUPS = len(POOL_WINDOWS)
POOL_GW = MIX_W // POOL_GROUPS
POOL_STATE = max(POOL_WINDOWS) - 1
B_HEADS = 8
B_HD = MIX_W // B_HEADS
DIL_PATTERNS = ((128, 1), (512, 4), (2048, 16))
WINDOW_MAX = max(w for w, _ in DIL_PATTERNS)
PEER_HEADS = 8
PEER_NKEYS = 128
PEER_EXPERTS = PEER_NKEYS * PEER_NKEYS
PEER_DK = 256
PEER_TOPK = 16
PAST_LEN = 2048
EPS = 1e-6
NEG = -1e30

LANES = 128
SUBLANES = 8
TOK_TILE = 512
PEER_TOK = 512
PEER_CHUNK_KEYS = 16
PEER_CHUNK = PEER_CHUNK_KEYS * PEER_NKEYS
PEER_PARTS = 4
PEER_PART = PEER_CHUNK // PEER_PARTS
PEER_STAIR = tuple(PEER_TOPK // (k + 1) for k in range(PEER_TOPK))
PEER_NCAND = -(-sum(PEER_STAIR) // SUBLANES) * SUBLANES
PEER_GUARD = 2.0 ** -21
POOL_HALO = 16
DIL_BLOCK = 128
DIL_WIN = 2048
SAMPLE_GROUP = 16
VMEM_LIMIT = 56 * 1024 * 1024


def _rms(x, g):
    r = lax.rsqrt(jnp.mean(x * x, axis=-1, keepdims=True) + EPS)
    return (x * r) * g


def _norm_matmul_kernel(x_ref, g_ref, w_ref, o_ref):
    h = _rms(x_ref[...], g_ref[...])
    o_ref[...] = jnp.dot(h.astype(BF16), w_ref[...], preferred_element_type=F32)


def _norm_matmul(x, g, w_bf16, tile=TOK_TILE):
    t, d = x.shape
    n = w_bf16.shape[1]
    tile = min(tile, t)
    assert t % tile == 0
    return pl.pallas_call(
        _norm_matmul_kernel,
        grid=(t // tile,),
        in_specs=[
            pl.BlockSpec((tile, d), lambda i: (i, 0)),
            pl.BlockSpec((1, d), lambda i: (0, 0)),
            pl.BlockSpec((d, n), lambda i: (0, 0)),
        ],
        out_specs=pl.BlockSpec((tile, n), lambda i: (i, 0)),
        out_shape=jax.ShapeDtypeStruct((t, n), F32),
        compiler_params=pltpu.CompilerParams(
            dimension_semantics=("parallel",), vmem_limit_bytes=VMEM_LIMIT),
        name="norm_matmul",
    )(x, g.reshape(1, d), w_bf16)


def _final_norm_kernel(x_ref, g_ref, o_ref):
    o_ref[...] = _rms(x_ref[...], g_ref[...])


def _final_norm(x, g, tile=TOK_TILE):
    t, d = x.shape
    assert t % tile == 0
    return pl.pallas_call(
        _final_norm_kernel,
        grid=(t // tile,),
        in_specs=[
            pl.BlockSpec((tile, d), lambda i: (i, 0)),
            pl.BlockSpec((1, d), lambda i: (0, 0)),
        ],
        out_specs=pl.BlockSpec((tile, d), lambda i: (i, 0)),
        out_shape=jax.ShapeDtypeStruct((t, d), F32),
        compiler_params=pltpu.CompilerParams(dimension_semantics=("parallel",)),
        name="final_norm",
    )(x, g.reshape(1, d))


def _extract_top(s, out_ref, n):
    for k in range(n):
        m = jnp.max(s, axis=0, keepdims=True)
        out_ref[k:k + 1, :] = m
        s = jnp.where(s == m, -jnp.inf, s)


def _peer_kernel(x_ref, g_ref, wqt_ref, sk_ref, u_ref, vt_ref, o_ref,
                 ht_s, q_s, se_s, thr_s, e1r_s, a_s, b_s, cand_s, top_s,
                 sc_s, w_s, acc_s):
    c = pl.program_id(1)
    tok = x_ref.shape[0]
    n_lt = tok // LANES

    @pl.when(c == 0)
    def _route():
        h = _rms(x_ref[...], g_ref[...])
        ht_s[...] = h.T.astype(BF16)
        acc_s[...] = jnp.zeros_like(acc_s)
        cand_s[...] = jnp.full(cand_s.shape, -jnp.inf, F32)

        def head_body(hd, carry):
            base = pl.multiple_of(hd * PEER_DK, PEER_DK)
            half = PEER_DK // 2
            row0 = pl.multiple_of(hd * PEER_NKEYS, PEER_NKEYS)
            q_s[...] = jnp.dot(wqt_ref[pl.ds(base, PEER_DK), :], ht_s[...],
                               preferred_element_type=F32).astype(BF16)
            s1 = jnp.dot(sk_ref[0], q_s[0:half, :], preferred_element_type=F32)
            s2 = jnp.dot(sk_ref[1], q_s[half:PEER_DK, :], preferred_element_type=F32)
            for lt in range(n_lt):
                ln = slice(lt * LANES, (lt + 1) * LANES)
                _extract_top(s1[:, ln], a_s, PEER_TOPK)
                _extract_top(s2[:, ln], b_s, PEER_TOPK)
                off = 0
                for k, width in enumerate(PEER_STAIR):
                    cand_s[off:off + width, :] = a_s[k:k + 1, :] + b_s[0:width, :]
                    off += width
                _extract_top(cand_s[...], top_s, PEER_TOPK)
                top = top_s[...]
                m0 = top[0:1, :]
                den = jnp.sum(jnp.exp(top - m0), axis=0, keepdims=True)
                tau = top[PEER_TOPK - 1:PEER_TOPK, :]
                s1_t = s1[:, ln]
                sc_s[pl.ds(row0, PEER_NKEYS), ln] = (
                    (tau - s1_t) - PEER_GUARD * (jnp.abs(tau) + jnp.abs(s1_t)))
                sc_s[pl.ds(PEER_HEADS * PEER_NKEYS + row0, PEER_NKEYS), ln] = (
                    jnp.exp(s1[:, ln] - a_s[0:1, :]) / den)
                se_s[lt, hd, 0, 0:PEER_NKEYS, :] = s2[:, ln]
                se_s[lt, hd, 1, 0:PEER_NKEYS, :] = jnp.exp(s2[:, ln] - b_s[0:1, :])
            return carry

        lax.fori_loop(0, PEER_HEADS, head_body, 0)

        for hd in range(PEER_HEADS):
            rows = pl.ds(hd, PEER_NKEYS, stride=PEER_HEADS)
            for lt in range(n_lt):
                ln = slice(lt * LANES, (lt + 1) * LANES)
                thr_s[lt, rows, :] = sc_s[hd * PEER_NKEYS:(hd + 1) * PEER_NKEYS, ln]
                e1r_s[lt, rows, :] = sc_s[(PEER_HEADS + hd) * PEER_NKEYS:
                                          (PEER_HEADS + hd + 1) * PEER_NKEYS, ln]

    def pre_dot(part):
        rows = slice(part * PEER_PART, (part + 1) * PEER_PART)
        sc_s[rows, :] = jnp.dot(u_ref[rows, :], ht_s[...], preferred_element_type=F32)

    def build(part):
        for k in range(PEER_PART // PEER_NKEYS):
            j = part * (PEER_PART // PEER_NKEYS) + k
            i1 = c * PEER_CHUNK_KEYS + j
            r0 = pl.multiple_of(i1 * PEER_HEADS, PEER_HEADS)
            rows = slice(j * PEER_NKEYS, (j + 1) * PEER_NKEYS)
            for lt in range(n_lt):
                ln = slice(lt * LANES, (lt + 1) * LANES)
                wgt = jnp.zeros((PEER_NKEYS, LANES), F32)
                thr_rows = thr_s[lt, pl.ds(r0, PEER_HEADS), :]
                e1_rows = e1r_s[lt, pl.ds(r0, PEER_HEADS), :]
                for hd in range(PEER_HEADS):
                    s2 = se_s[lt, hd, 0, 0:PEER_NKEYS, :]
                    p = e1_rows[hd:hd + 1, :] * se_s[lt, hd, 1, 0:PEER_NKEYS, :]
                    wgt = wgt + jnp.where(s2 >= thr_rows[hd:hd + 1, :], p, 0.0)
                pre = sc_s[rows, ln]
                act = 0.5 * pre * (1.0 + lax.erf(pre * (1.0 / math.sqrt(2.0))))
                w_s[rows, ln] = (wgt * act).astype(BF16)

    def out_dot(part):
        rows = slice(part * PEER_PART, (part + 1) * PEER_PART)
        acc_s[...] += jnp.dot(vt_ref[:, rows], w_s[rows, :], preferred_element_type=F32)

    pre_dot(0)
    for part in range(PEER_PARTS):
        if part + 1 < PEER_PARTS:
            pre_dot(part + 1)
        build(part)
        if part > 0:
            out_dot(part - 1)
    out_dot(PEER_PARTS - 1)

    @pl.when(c == pl.num_programs(1) - 1)
    def _finish():
        o_ref[...] = x_ref[...] + acc_s[...].T


def _peer(x, g, wqt_bf16, sk_bf16, u_bf16, vt_bf16, tok=PEER_TOK):
    t, d = x.shape
    assert t % tok == 0 and 2 * PEER_HEADS * PEER_NKEYS <= PEER_CHUNK
    n_chunks = PEER_EXPERTS // PEER_CHUNK
    hq = PEER_HEADS * PEER_DK
    return pl.pallas_call(
        _peer_kernel,
        grid=(t // tok, n_chunks),
        in_specs=[
            pl.BlockSpec((tok, d), lambda i, c: (i, 0)),
            pl.BlockSpec((1, d), lambda i, c: (0, 0)),
            pl.BlockSpec((hq, d), lambda i, c: (0, 0), pipeline_mode=pl.Buffered(1)),
            pl.BlockSpec((2, PEER_NKEYS, PEER_DK // 2), lambda i, c: (0, 0, 0)),
            pl.BlockSpec((PEER_CHUNK, d), lambda i, c: (c, 0)),
            pl.BlockSpec((d, PEER_CHUNK), lambda i, c: (0, c)),
        ],
        out_specs=pl.BlockSpec((tok, d), lambda i, c: (i, 0)),
        out_shape=jax.ShapeDtypeStruct((t, d), F32),
        scratch_shapes=[
            pltpu.VMEM((d, tok), BF16),
            pltpu.VMEM((PEER_DK, tok), BF16),
            pltpu.VMEM((tok // LANES, PEER_HEADS, 2, PEER_NKEYS + SUBLANES, LANES), F32),
            pltpu.VMEM((tok // LANES, PEER_NKEYS * PEER_HEADS, LANES), F32),
            pltpu.VMEM((tok // LANES, PEER_NKEYS * PEER_HEADS, LANES), F32),
            pltpu.VMEM((PEER_TOPK, LANES), F32),
            pltpu.VMEM((PEER_TOPK, LANES), F32),
            pltpu.VMEM((PEER_NCAND, LANES), F32),
            pltpu.VMEM((PEER_TOPK, LANES), F32),
            pltpu.VMEM((PEER_CHUNK, tok), F32),
            pltpu.VMEM((PEER_CHUNK, tok), BF16),
            pltpu.VMEM((d, tok), F32),
        ],
        compiler_params=pltpu.CompilerParams(
            dimension_semantics=("parallel", "arbitrary"), vmem_limit_bytes=VMEM_LIMIT),
        name="peer",
    )(x, g.reshape(1, d), wqt_bf16, sk_bf16, u_bf16, vt_bf16)


def _pool_tile_plan(ct):
    lo = ct * LANES
    hi = lo + LANES - 1
    return [(POOL_WINDOWS[g], (g + 1) * POOL_GW) for g in range(lo // POOL_GW, hi // POOL_GW + 1)]


def _pool_diff_tile(load_shifted, z_tile, pos1, ct):
    plan = _pool_tile_plan(ct)
    wanted = {w for w, _ in plan}
    acc = z_tile
    snaps = {}
    for j in range(1, max(wanted)):
        acc = acc + load_shifted(j)
        if j + 1 in wanted:
            snaps[j + 1] = acc
    if len(plan) == 1:
        w = plan[0][0]
        win = snaps[w]
        cnt = jnp.minimum(float(w), pos1)
    else:
        (w_lo, edge), (w_hi, _) = plan
        lane = lax.broadcasted_iota(jnp.int32, (1,) * (z_tile.ndim - 1) + (LANES,), z_tile.ndim - 1)
        in_lo = lane + ct * LANES < edge
        win = jnp.where(in_lo, snaps[w_lo], snaps[w_hi])
        cnt = jnp.minimum(jnp.where(in_lo, float(w_lo), float(w_hi)), pos1)
    return win / cnt - z_tile


def _pool_prompt_kernel(z_ref, wbd_ref, scale_ref, mix_ref, state_ref, zc_s, d_s):
    i = pl.program_id(1)
    tile = z_ref.shape[0]

    @pl.when(i == 0)
    def _start():
        zc_s[0:POOL_HALO, :] = jnp.zeros((POOL_HALO, MIX_W), F32)

    zc_s[POOL_HALO:POOL_HALO + tile, :] = z_ref[...]
    pos1 = (i * tile + lax.broadcasted_iota(jnp.int32, (tile, 1), 0) + 1).astype(F32)
    for ct in range(MIX_W // LANES):
        ln = slice(ct * LANES, (ct + 1) * LANES)
        diff = _pool_diff_tile(
            lambda j: zc_s[POOL_HALO - j:POOL_HALO - j + tile, ln], z_ref[:, ln], pos1, ct)
        d_s[:, ln] = diff.astype(BF16)
    mix_ref[...] = jnp.dot(d_s[...], wbd_ref[...], preferred_element_type=F32) * scale_ref[...]
    last = z_ref[tile - POOL_HALO:tile, :]
    state_ref[...] = last
    zc_s[0:POOL_HALO, :] = last


def _pool_prompt(proj, nb, seq, wbd_bf16, scale, tile=TOK_TILE):
    nt = seq // tile
    assert seq % tile == 0 and tile >= POOL_HALO
    return pl.pallas_call(
        _pool_prompt_kernel,
        grid=(nb, nt),
        in_specs=[
            pl.BlockSpec((tile, MIX_W), lambda n, i: (n * nt + i, 0)),
            pl.BlockSpec((MIX_W, MIX_W), lambda n, i: (0, 0)),
            pl.BlockSpec((1, MIX_W), lambda n, i: (0, 0)),
        ],
        out_specs=[
            pl.BlockSpec((tile, MIX_W), lambda n, i: (n * nt + i, 0)),
            pl.BlockSpec((None, POOL_HALO, MIX_W), lambda n, i: (n, 0, 0)),
        ],
        out_shape=[
            jax.ShapeDtypeStruct((nb * seq, MIX_W), F32),
            jax.ShapeDtypeStruct((nb, POOL_HALO, MIX_W), F32),
        ],
        scratch_shapes=[
            pltpu.VMEM((POOL_HALO + tile, MIX_W), F32),
            pltpu.VMEM((tile, MIX_W), BF16),
        ],
        compiler_params=pltpu.CompilerParams(
            dimension_semantics=("arbitrary", "arbitrary"), vmem_limit_bytes=VMEM_LIMIT),
        name="pool_prompt",
    )(proj, wbd_bf16, scale.reshape(1, MIX_W))


def _pool_sample_kernel(z_ref, st_ref, wbd_ref, scale_ref, mix_ref, nst_ref, zc_s, d_s, *, pos0):
    grp, n_state, _ = st_ref.shape
    t = z_ref.shape[0] // grp
    zc_s[:, POOL_HALO - n_state:POOL_HALO, :] = st_ref[...]
    zc_s[:, POOL_HALO:POOL_HALO + t, :] = z_ref[...].reshape(grp, t, MIX_W)
    pos1 = (pos0 + lax.broadcasted_iota(jnp.int32, (1, t, 1), 1) + 1).astype(F32)
    for ct in range(MIX_W // LANES):
        ln = slice(ct * LANES, (ct + 1) * LANES)
        diff = _pool_diff_tile(
            lambda j: zc_s[:, POOL_HALO - j:POOL_HALO - j + t, ln],
            zc_s[:, POOL_HALO:POOL_HALO + t, ln], pos1, ct)
        d_s[:, ln] = diff.reshape(grp * t, LANES).astype(BF16)
    mix_ref[...] = jnp.dot(d_s[...], wbd_ref[...], preferred_element_type=F32) * scale_ref[...]
    nst_ref[...] = zc_s[:, POOL_HALO + t - n_state:POOL_HALO + t, :]


def _pool_sample(proj, nb, t, state, wbd_bf16, scale, pos0, grp=SAMPLE_GROUP):
    n_state = state.shape[1]
    assert nb % grp == 0 and t % SUBLANES == 0
    assert max(POOL_WINDOWS) - 1 <= n_state <= POOL_HALO - 1
    return pl.pallas_call(
        functools.partial(_pool_sample_kernel, pos0=pos0),
        grid=(nb // grp,),
        in_specs=[
            pl.BlockSpec((grp * t, MIX_W), lambda i: (i, 0)),
            pl.BlockSpec((grp, n_state, MIX_W), lambda i: (i, 0, 0)),
            pl.BlockSpec((MIX_W, MIX_W), lambda i: (0, 0)),
            pl.BlockSpec((1, MIX_W), lambda i: (0, 0)),
        ],
        out_specs=[
            pl.BlockSpec((grp * t, MIX_W), lambda i: (i, 0)),
            pl.BlockSpec((grp, n_state, MIX_W), lambda i: (i, 0, 0)),
        ],
        out_shape=[
            jax.ShapeDtypeStruct((nb * t, MIX_W), F32),
            jax.ShapeDtypeStruct((nb, n_state, MIX_W), F32),
        ],
        scratch_shapes=[
            pltpu.VMEM((grp, POOL_HALO + t, MIX_W), F32),
            pltpu.VMEM((grp * t, MIX_W), BF16),
        ],
        compiler_params=pltpu.CompilerParams(
            dimension_semantics=("parallel",), vmem_limit_bytes=VMEM_LIMIT),
        name="pool_sample",
    )(proj, state, wbd_bf16, scale.reshape(1, MIX_W))


def _mem_out_kernel(x_ref, mix_ref, qm_ref, mk_ref, mv_ref, wmix_ref, wmem_ref, o_ref):
    grp = mk_ref.shape[0]
    rows = x_ref.shape[0]
    tq = rows // grp
    acc = x_ref[...] + jnp.dot(mix_ref[...].astype(BF16), wmix_ref[...],
                               preferred_element_type=F32)
    for h in range(MEM_HEADS):
        cols = slice(h * MEM_HD, (h + 1) * MEM_HD)
        q = qm_ref[:, cols].astype(BF16).reshape(grp, tq, MEM_HD)
        k = mk_ref[:, :, cols].astype(BF16)
        v = mv_ref[:, :, cols].astype(BF16)
        s = jnp.einsum('bqd,bkd->bqk', q, k, preferred_element_type=F32) * (MEM_HD ** -0.5)
        e = jnp.exp(s - jnp.max(s, axis=-1, keepdims=True))
        p = e / jnp.sum(e, axis=-1, keepdims=True)
        o = jnp.einsum('bqk,bkd->bqd', p.astype(BF16), v, preferred_element_type=F32)
        acc = acc + jnp.dot(o.reshape(rows, MEM_HD).astype(BF16), wmem_ref[cols, :],
                            preferred_element_type=F32)
    o_ref[...] = acc


def _mem_out(x, mix, proj, qm_col_blk, mem_k, mem_v, wmix_bf16, wmem_bf16, grp, tq):
    rows, d = x.shape
    kmix = mix.shape[1]
    n_seq, n_mem, _ = mem_k.shape
    step = grp * tq
    per_seq = rows // n_seq
    assert rows % step == 0 and tq % SUBLANES == 0
    assert (grp == 1 and per_seq % tq == 0) or per_seq == tq
    tiles_per_seq = per_seq // tq
    seq_blk = (lambda i: i // tiles_per_seq) if grp == 1 else (lambda i: i)
    return pl.pallas_call(
        _mem_out_kernel,
        grid=(rows // step,),
        in_specs=[
            pl.BlockSpec((step, d), lambda i: (i, 0)),
            pl.BlockSpec((step, kmix), lambda i: (i, 0)),
            pl.BlockSpec((step, MEM_W), lambda i: (i, qm_col_blk)),
            pl.BlockSpec((grp, n_mem, MEM_W), lambda i: (seq_blk(i), 0, 0)),
            pl.BlockSpec((grp, n_mem, MEM_W), lambda i: (seq_blk(i), 0, 0)),
            pl.BlockSpec((kmix, d), lambda i: (0, 0)),
            pl.BlockSpec((MEM_W, d), lambda i: (0, 0)),
        ],
        out_specs=pl.BlockSpec((step, d), lambda i: (i, 0)),
        out_shape=jax.ShapeDtypeStruct((rows, d), F32),
        compiler_params=pltpu.CompilerParams(
            dimension_semantics=("parallel",), vmem_limit_bytes=VMEM_LIMIT),
        name="mem_out",
    )(x, mix, proj, mem_k, mem_v, wmix_bf16, wmem_bf16)


def _dil_prompt_kernel(slope_ref, q_ref, kc_ref, kp_ref, vc_ref, vp_ref, o_ref, og_s, lse_s):
    win = pl.program_id(1)
    rows = q_ref.shape[0]
    blk = DIL_BLOCK
    slope = slope_ref[...]
    ri = lax.broadcasted_iota(jnp.int32, (blk, blk), 0)
    ci = lax.broadcasted_iota(jnp.int32, (blk, blk), 1)
    diff = (ri - ci).astype(F32)
    own_ok = (ri >= ci)[None]
    prev_ok = (ci >= ri)[None]
    not_first = win > 0
    scale = B_HD ** -0.5

    for g, (w, d) in enumerate(DIL_PATTERNS):
        per_res = rows // (blk * d)
        assert w // d == blk and rows % (blk * d) == 0 and per_res & (per_res - 1) == 0

        def rows_of(r, j):
            return pl.ds(j * blk * d + r, blk, stride=d) if d > 1 else pl.ds(j * blk, blk)

        own = [rows_of(r, j) for r in range(d) for j in range(per_res)]
        prv = [(kc_ref, vc_ref, rows_of(r, j - 1)) if j > 0
               else (kp_ref, vp_ref, rows_of(r, per_res - 1))
               for r in range(d) for j in range(per_res)]
        q = jnp.stack([q_ref[o, :] for o in own]).astype(BF16)
        k_own = jnp.stack([kc_ref[o, :] for o in own]).astype(BF16)
        v_own = jnp.stack([vc_ref[o, :] for o in own]).astype(BF16)
        k_prev = jnp.stack([kr[o, :] for kr, _, o in prv]).astype(BF16)
        v_prev = jnp.stack([vr[o, :] for _, vr, o in prv]).astype(BF16)
        bias_own = (-(slope * float(d)) * diff)[None]
        bias_prev = (-(slope * float(d)) * (diff + float(blk)))[None]
        s_own = jnp.einsum('bqd,bkd->bqk', q, k_own, preferred_element_type=F32) * scale + bias_own
        s_own = jnp.where(own_ok, s_own, NEG)
        s_prev = jnp.einsum('bqd,bkd->bqk', q, k_prev, preferred_element_type=F32) * scale + bias_prev
        bi = lax.broadcasted_iota(jnp.int32, (len(own), blk, blk), 0)
        has_prev = jnp.logical_or((bi & (per_res - 1)) != 0, not_first)
        s_prev = jnp.where(jnp.logical_and(prev_ok, has_prev), s_prev, NEG)
        m = jnp.maximum(jnp.max(s_own, axis=-1, keepdims=True),
                        jnp.max(s_prev, axis=-1, keepdims=True))
        e_own = jnp.exp(s_own - m)
        e_prev = jnp.exp(s_prev - m)
        den = jnp.sum(e_own, axis=-1, keepdims=True) + jnp.sum(e_prev, axis=-1, keepdims=True)
        o = (jnp.einsum('bqk,bkd->bqd', (e_own / den).astype(BF16), v_own,
                        preferred_element_type=F32)
             + jnp.einsum('bqk,bkd->bqd', (e_prev / den).astype(BF16), v_prev,
                          preferred_element_type=F32))
        lse = jnp.broadcast_to(m + jnp.log(den), o.shape)
        for b, rows_b in enumerate(own):
            og_s[g, rows_b, :] = o[b]
            lse_s[g, rows_b, :] = lse[b]

    top = jnp.maximum(jnp.maximum(lse_s[0], lse_s[1]), lse_s[2])
    wts = [jnp.exp(lse_s[g] - top) for g in range(len(DIL_PATTERNS))]
    tot = wts[0] + wts[1] + wts[2]
    o_ref[...] = ((wts[0] / tot) * og_s[0] + (wts[1] / tot) * og_s[1] + (wts[2] / tot) * og_s[2])


def _dil_prompt(proj_pad, kv_pad, nb, seq, slopes_b):
    win = DIL_WIN
    nw = seq // win
    assert seq % win == 0 and len(DIL_PATTERNS) == 3
    cur = lambda col0: (lambda n, w, h: (n * nw + w, col0 + h))
    prev = lambda col0: (lambda n, w, h: (n * nw + jnp.maximum(w - 1, 0), col0 + h))
    return pl.pallas_call(
        _dil_prompt_kernel,
        grid=(nb, nw, B_HEADS),
        in_specs=[
            pl.BlockSpec((None, 1, LANES), lambda n, w, h: (h, 0, 0)),
            pl.BlockSpec((win, LANES), cur(0)),
            pl.BlockSpec((win, LANES), cur(0)),
            pl.BlockSpec((win, LANES), prev(0)),
            pl.BlockSpec((win, LANES), cur(B_HEADS)),
            pl.BlockSpec((win, LANES), prev(B_HEADS)),
        ],
        out_specs=pl.BlockSpec((win, LANES), cur(0)),
        out_shape=jax.ShapeDtypeStruct((nb * seq, B_HEADS * LANES), F32),
        scratch_shapes=[
            pltpu.VMEM((len(DIL_PATTERNS), win, LANES), F32),
            pltpu.VMEM((len(DIL_PATTERNS), win, LANES), F32),
        ],
        compiler_params=pltpu.CompilerParams(
            dimension_semantics=("parallel", "parallel", "parallel"),
            vmem_limit_bytes=VMEM_LIMIT),
        name="dil_prompt",
    )(slopes_b, proj_pad, kv_pad, kv_pad, kv_pad, kv_pad)


def _dil_sample_kernel(q_ref, kc_ref, vc_ref, kn_ref, vn_ref, bc_ref, bn_ref, hm_ref, o_ref,
                       knp_s, vnp_s, pc_s, *, row_lo):
    t = q_ref.shape[0]
    nt = (((1,), (1,)), ((), ()))
    scale = B_HD ** -0.5
    hm = hm_ref[...]
    qbd = (jnp.concatenate([q_ref[...]] * B_HEADS, axis=0) * hm).astype(BF16)
    knp_s[...] = jnp.zeros_like(knp_s)
    vnp_s[...] = jnp.zeros_like(vnp_s)
    knp_s[0:t, :] = kn_ref[...]
    vnp_s[0:t, :] = vn_ref[...]
    s_c = lax.dot_general(qbd, kc_ref[...].astype(BF16), nt, preferred_element_type=F32) * scale
    s_n = lax.dot_general(qbd, knp_s[...].astype(BF16), nt, preferred_element_type=F32) * scale
    stats = []
    for g in range(len(DIL_PATTERNS)):
        lo = row_lo[g]
        bc = bc_ref[g, :, lo:]
        bn = bn_ref[g]
        sc = jnp.where(bc > 0.5 * NEG, s_c[:, lo:] + bc, NEG)
        sn = jnp.where(bn > 0.5 * NEG, s_n + bn, NEG)
        m = jnp.maximum(jnp.max(sc, axis=-1, keepdims=True), jnp.max(sn, axis=-1, keepdims=True))
        ec = jnp.exp(sc - m)
        en = jnp.exp(sn - m)
        den = jnp.sum(ec, axis=-1, keepdims=True) + jnp.sum(en, axis=-1, keepdims=True)
        stats.append((ec / den, en / den, m + jnp.log(den)))
    top = jnp.maximum(jnp.maximum(stats[0][2], stats[1][2]), stats[2][2])
    wts = [jnp.exp(st[2] - top) for st in stats]
    tot = wts[0] + wts[1] + wts[2]
    pc_s[...] = jnp.zeros_like(pc_s)
    pn = jnp.zeros((B_HEADS * t, knp_s.shape[0]), F32)
    for g in range(len(DIL_PATTERNS)):
        lo = row_lo[g]
        alpha = wts[g] / tot
        pc_s[:, lo:] += alpha * stats[g][0]
        pn = pn + alpha * stats[g][1]
    o = (jnp.dot(pc_s[...].astype(BF16), vc_ref[...].astype(BF16), preferred_element_type=F32)
         + jnp.dot(pn.astype(BF16), vnp_s[...].astype(BF16), preferred_element_type=F32))
    o_ref[...] = jnp.sum((o * hm).reshape(B_HEADS, t, MIX_W), axis=0)


def _dil_sample_tables(past, t):
    slopes = np.asarray([2.0 ** (-8.0 * (h + 1) / B_HEADS) for h in range(B_HEADS)], np.float32)
    qpos = past + np.arange(t)
    col = np.arange(past + LANES)
    exists = col < past + t
    delta = qpos[:, None] - col[None, :]
    bias = np.full((len(DIL_PATTERNS), B_HEADS, t, past + LANES), NEG, np.float32)
    col_lo = []
    for g, (w, d) in enumerate(DIL_PATTERNS):
        ok = (delta >= 0) & (delta <= w) & (delta % d == 0) & exists[None, :]
        vals = -slopes[:, None, None] * delta[None].astype(np.float32)
        bias[g] = np.where(ok[None], vals, np.float32(NEG))
        col_lo.append(max(0, (past - w) // LANES * LANES))
    bias = bias.reshape(len(DIL_PATTERNS), B_HEADS * t, past + LANES)
    hm = (np.arange(MIX_W)[None, :] // B_HD == np.arange(B_HEADS * t)[:, None] // t)
    return bias[:, :, :past], bias[:, :, past:], hm.astype(np.float32), tuple(col_lo)


def _dil_sample(proj, kv_new, cache_k, cache_v):
    nb, past, _ = cache_k.shape
    t = proj.shape[0] // nb
    assert t % SUBLANES == 0 and t <= LANES and past % LANES == 0 and len(DIL_PATTERNS) == 3
    bias_c, bias_n, hm, col_lo = _dil_sample_tables(past, t)
    ng = len(DIL_PATTERNS)
    return pl.pallas_call(
        functools.partial(_dil_sample_kernel, row_lo=col_lo),
        grid=(nb,),
        in_specs=[
            pl.BlockSpec((t, MIX_W), lambda b: (b, 0)),
            pl.BlockSpec((None, past, MIX_W), lambda b: (b, 0, 0)),
            pl.BlockSpec((None, past, MIX_W), lambda b: (b, 0, 0)),
            pl.BlockSpec((t, MIX_W), lambda b: (b, 0)),
            pl.BlockSpec((t, MIX_W), lambda b: (b, 1)),
            pl.BlockSpec((ng, B_HEADS * t, past), lambda b: (0, 0, 0)),
            pl.BlockSpec((ng, B_HEADS * t, LANES), lambda b: (0, 0, 0)),
            pl.BlockSpec((B_HEADS * t, MIX_W), lambda b: (0, 0)),
        ],
        out_specs=pl.BlockSpec((t, MIX_W), lambda b: (b, 0)),
        out_shape=jax.ShapeDtypeStruct((nb * t, MIX_W), F32),
        scratch_shapes=[
            pltpu.VMEM((LANES, MIX_W), F32),
            pltpu.VMEM((LANES, MIX_W), F32),
            pltpu.VMEM((B_HEADS * t, past), F32),
        ],
        compiler_params=pltpu.CompilerParams(
            dimension_semantics=("parallel",), vmem_limit_bytes=VMEM_LIMIT),
        name="dil_sample",
    )(proj, cache_k, cache_v, kv_new, kv_new, jnp.asarray(bias_c), jnp.asarray(bias_n),
      jnp.asarray(hm))


def _pad_head_cols(w):
    d = w.shape[0]
    w = jnp.pad(w.reshape(d, B_HEADS, B_HD), ((0, 0), (0, 0), (0, LANES - B_HD)))
    return w.reshape(d, B_HEADS * LANES)


def _unpad_heads(a):
    return a.reshape(a.shape[:-1] + (B_HEADS, LANES))[..., :B_HD]


def kernel(x_prompt, x_sample, state_pool, cache_win_k, cache_win_v, cache_mem_k, cache_mem_v,
           mem_prompt, norm_mix, w_in, pool_w, pool_scale, norm_mem, w_mem_kv, w_out,
           norm_kv, w_kv, norm_ffn, peer_wq, peer_subkeys, peer_u, peer_v, norm_final):
    nb_p, seq, d = x_prompt.shape
    nb_s, dec_seq, _ = x_sample.shape
    n_mem = mem_prompt.shape[1]
    past = cache_win_k.shape[1]
    tp = nb_p * seq
    ts = nb_s * dec_seq

    mem_flat = mem_prompt.reshape(nb_p * n_mem, d)
    mkv = [_norm_matmul(mem_flat, norm_mem[l], w_mem_kv[l].astype(BF16)) for l in range(DEPTH)]
    mem_k_p = jnp.stack([m[:, :MEM_W].reshape(nb_p, n_mem, MEM_W) for m in mkv], axis=0)
    mem_v_p = jnp.stack([m[:, MEM_W:].reshape(nb_p, n_mem, MEM_W) for m in mkv], axis=0)
    mem_k_s = cache_mem_k.reshape(DEPTH, nb_s, n_mem, MEM_W)
    mem_v_s = cache_mem_v.reshape(DEPTH, nb_s, n_mem, MEM_W)

    slopes_b = jnp.broadcast_to(
        jnp.asarray([2.0 ** (-8.0 * (h + 1) / B_HEADS) for h in range(B_HEADS)], F32)[:, None, None],
        (B_HEADS, 1, LANES))
    cache_k = cache_win_k.reshape(nb_s, past, MIX_W)
    cache_v = cache_win_v.reshape(nb_s, past, MIX_W)

    xp = x_prompt.reshape(tp, d)
    xs = x_sample.reshape(ts, d)
    pool_p, pool_s = [], []
    kv_p = kv_s = None
    for l in range(DEPTH):
        w_in_l = w_in[l].astype(BF16)
        w_out_l = w_out[l].astype(BF16)
        proj_s = _norm_matmul(xs, norm_mix[l], w_in_l)
        if l < N_A:
            proj_p = _norm_matmul(xp, norm_mix[l], w_in_l)
            wbd = jax.scipy.linalg.block_diag(*[pool_w[l, g] for g in range(POOL_GROUPS)]).astype(BF16)
            mix_p, last_p = _pool_prompt(proj_p, nb_p, seq, wbd, pool_scale[l])
            pool_p.append(last_p[:, POOL_HALO - POOL_STATE:])
            mix_s, new_state = _pool_sample(proj_s, nb_s, dec_seq, state_pool[l], wbd,
                                            pool_scale[l], PAST_LEN)
            pool_s.append(new_state)
            wmix_p = w_out_l[:MIX_W]
            qm_blk_p = MIX_W // MEM_W
        else:
            w_in_pad = jnp.concatenate([_pad_head_cols(w_in[l][:, :MIX_W]), w_in[l][:, MIX_W:]],
                                       axis=1).astype(BF16)
            proj_p = _norm_matmul(xp, norm_mix[l], w_in_pad)
            mix_p = _dil_prompt(proj_p, kv_p, nb_p, seq, slopes_b)
            mix_s = _dil_sample(proj_s, kv_s, cache_k, cache_v)
            wmix_p = _pad_head_cols(w_out[l][:MIX_W].T).T.astype(BF16)
            qm_blk_p = B_HEADS * LANES // MEM_W
        xp = _mem_out(xp, mix_p, proj_p, qm_blk_p, mem_k_p[l], mem_v_p[l], wmix_p,
                      w_out_l[MIX_W:], 1, TOK_TILE)
        xs = _mem_out(xs, mix_s, proj_s, MIX_W // MEM_W, mem_k_s[l], mem_v_s[l], w_out_l[:MIX_W],
                      w_out_l[MIX_W:], SAMPLE_GROUP, dec_seq)
        peer_w = (norm_ffn[l], peer_wq[l].T.astype(BF16), peer_subkeys[l].astype(BF16),
                  peer_u[l].astype(BF16), peer_v[l].T.astype(BF16))
        xp = _peer(xp, *peer_w)
        xs = _peer(xs, *peer_w)
        if l == N_A - 1:
            w_kv_pad = jnp.concatenate([_pad_head_cols(w_kv[:, :MIX_W]),
                                        _pad_head_cols(w_kv[:, MIX_W:])], axis=1).astype(BF16)
            kv_p = _norm_matmul(xp, norm_kv, w_kv_pad)
            kv_s = _norm_matmul(xs, norm_kv, w_kv.astype(BF16))
    y_p = _final_norm(xp, norm_final)
    y_s = _final_norm(xs, norm_final)
    keep = min(WINDOW_MAX, seq)
    kv_p4 = kv_p.reshape(nb_p, seq, 2 * B_HEADS * LANES)[:, seq - keep:]
    return (y_p.reshape(nb_p, seq, d), y_s.reshape(nb_s, dec_seq, d),
            jnp.stack(pool_p, axis=0),
            _unpad_heads(kv_p4[..., :B_HEADS * LANES]), _unpad_heads(kv_p4[..., B_HEADS * LANES:]),
            mem_k_p.reshape(DEPTH, nb_p, n_mem, MEM_HEADS, MEM_HD),
            mem_v_p.reshape(DEPTH, nb_p, n_mem, MEM_HEADS, MEM_HD),
            jnp.stack(pool_s, axis=0),
            kv_s[:, :MIX_W].reshape(nb_s, dec_seq, B_HEADS, B_HD),
            kv_s[:, MIX_W:].reshape(nb_s, dec_seq, B_HEADS, B_HD))
```

```python
import functools
import math

import numpy as np
import jax
import jax.numpy as jnp
from jax import lax
from jax.experimental import pallas as pl
from jax.experimental.pallas import tpu as pltpu

F32 = jnp.float32
BF16 = jnp.bfloat16

D_MODEL = 1024
DEPTH = 4
N_A = DEPTH // 2
MIX_W = 3 * D_MODEL // 4
MEM_HEADS = 4
MEM_HD = (D_MODEL - MIX_W) // MEM_HEADS
MEM_W = MEM_HEADS * MEM_HD
POOL_WINDOWS = (2, 4, 8, 16)
POOL_GROUPS = len(POOL_WINDOWS)
POOL_GW = MIX_W // POOL_GROUPS
POOL_STATE = max(POOL_WINDOWS) - 1
B_HEADS = 8
B_HD = MIX_W // B_HEADS
DIL_PATTERNS = ((128, 1), (512, 4), (2048, 16))
WINDOW_MAX = max(w for w, _ in DIL_PATTERNS)
PEER_HEADS = 8
PEER_NKEYS = 128
PEER_EXPERTS = PEER_NKEYS * PEER_NKEYS
PEER_DK = 256
PEER_TOPK = 16
PAST_LEN = 2048
EPS = 1e-6
NEG = -1e30

LANES = 128
SUBLANES = 8
TOK_TILE = 512
PEER_TOK = 512
PEER_CHUNK_KEYS = 16
PEER_CHUNK = PEER_CHUNK_KEYS * PEER_NKEYS
PEER_PART_KEYS = (1, 3, 6, 5, 1)
PEER_STAIR = tuple(PEER_TOPK // (k + 1) for k in range(PEER_TOPK))
PEER_NCAND = -(-sum(PEER_STAIR) // SUBLANES) * SUBLANES
PEER_GUARD = 2.0 ** -21
POOL_HALO = 16
DIL_BLOCK = 128
DIL_WIN = 2048
SAMPLE_GROUP = 16
VMEM_LIMIT = 56 * 1024 * 1024


def _rms(x, g):
    r = lax.rsqrt(jnp.mean(x * x, axis=-1, keepdims=True) + EPS)
    return (x * r) * g


def _norm_matmul_kernel(x_ref, g_ref, w_ref, o_ref):
    h = _rms(x_ref[...], g_ref[...])
    o_ref[...] = jnp.dot(h.astype(BF16), w_ref[...], preferred_element_type=F32)


def _norm_matmul(x, g, w_bf16, tile=TOK_TILE):
    t, d = x.shape
    n = w_bf16.shape[1]
    tile = min(tile, t)
    assert t % tile == 0
    return pl.pallas_call(
        _norm_matmul_kernel,
        grid=(t // tile,),
        in_specs=[
            pl.BlockSpec((tile, d), lambda i: (i, 0)),
            pl.BlockSpec((1, d), lambda i: (0, 0)),
            pl.BlockSpec((d, n), lambda i: (0, 0)),
        ],
        out_specs=pl.BlockSpec((tile, n), lambda i: (i, 0)),
        out_shape=jax.ShapeDtypeStruct((t, n), F32),
        compiler_params=pltpu.CompilerParams(
            dimension_semantics=("parallel",), vmem_limit_bytes=VMEM_LIMIT),
        name="norm_matmul",
    )(x, g.reshape(1, d), w_bf16)


def _final_norm_kernel(x_ref, g_ref, o_ref):
    o_ref[...] = _rms(x_ref[...], g_ref[...])


def _final_norm(x, g, tile=TOK_TILE):
    t, d = x.shape
    assert t % tile == 0
    return pl.pallas_call(
        _final_norm_kernel,
        grid=(t // tile,),
        in_specs=[
            pl.BlockSpec((tile, d), lambda i: (i, 0)),
            pl.BlockSpec((1, d), lambda i: (0, 0)),
        ],
        out_specs=pl.BlockSpec((tile, d), lambda i: (i, 0)),
        out_shape=jax.ShapeDtypeStruct((t, d), F32),
        compiler_params=pltpu.CompilerParams(dimension_semantics=("parallel",)),
        name="final_norm",
    )(x, g.reshape(1, d))


def _extract_top(s, out_ref, n):
    for k in range(n):
        m = jnp.max(s, axis=0, keepdims=True)
        out_ref[k:k + 1, :] = m
        s = jnp.where(s == m, -jnp.inf, s)


def _odd_even_merge_sort(n):
    pairs = []
    p = 1
    while p < n:
        k = p
        while k >= 1:
            for j in range(k % p, n - k, 2 * k):
                for i in range(min(k, n - j - k)):
                    if (i + j) // (2 * p) == (i + j + k) // (2 * p):
                        pairs.append((i + j, i + j + k))
            k //= 2
        p *= 2
    return pairs


def _sorted_top(s, out_ref):
    n = s.shape[0] // SUBLANES
    assert s.shape[0] == n * SUBLANES and n & (n - 1) == 0

    def exchange(v, i, j):
        v[i], v[j] = jnp.maximum(v[i], v[j]), jnp.minimum(v[i], v[j])

    v = [s[i * SUBLANES:(i + 1) * SUBLANES, :] for i in range(n)]
    for i, j in _odd_even_merge_sort(n):
        exchange(v, i, j)
    shift = SUBLANES // 2
    while shift >= 1:
        other = [pltpu.roll(t, shift, 0) for t in v]
        v = [jnp.maximum(v[i], other[n - 1 - i]) for i in range(n)]
        stride = n // 2
        while stride >= 1:
            for i in range(n):
                if i & stride == 0:
                    exchange(v, i, i + stride)
            stride //= 2
        shift //= 2
    for k in range(n):
        out_ref[k:k + 1, :] = v[k][0:1, :]


def _peer_kernel(x_ref, g_ref, wqt_ref, sk_ref, u_ref, vt_ref, o_ref,
                 ht_s, q_s, se_s, thr_s, e1r_s, thr_c, e1_c, a_s, b_s, cand_s, top_s,
                 sc_s, w_s, acc_s):
    c = pl.program_id(1)
    tok = x_ref.shape[0]
    n_lt = tok // LANES

    @pl.when(c == 0)
    def _route():
        h = _rms(x_ref[...], g_ref[...])
        ht_s[...] = h.T.astype(BF16)
        cand_s[...] = jnp.full(cand_s.shape, -jnp.inf, F32)

        def head_body(hd, carry):
            base = pl.multiple_of(hd * PEER_DK, PEER_DK)
            half = PEER_DK // 2
            row0 = pl.multiple_of(hd * PEER_NKEYS, PEER_NKEYS)
            q_s[...] = jnp.dot(wqt_ref[pl.ds(base, PEER_DK), :], ht_s[...],
                               preferred_element_type=F32).astype(BF16)
            s1 = jnp.dot(sk_ref[0], q_s[0:half, :], preferred_element_type=F32)
            s2 = jnp.dot(sk_ref[1], q_s[half:PEER_DK, :], preferred_element_type=F32)
            for lt in range(n_lt):
                ln = slice(lt * LANES, (lt + 1) * LANES)
                _sorted_top(s1[:, ln], a_s)
                _sorted_top(s2[:, ln], b_s)
                off = 0
                for k, width in enumerate(PEER_STAIR):
                    cand_s[off:off + width, :] = a_s[k:k + 1, :] + b_s[0:width, :]
                    off += width
                _extract_top(cand_s[...], top_s, PEER_TOPK)
                top = top_s[...]
                m0 = top[0:1, :]
                den = jnp.sum(jnp.exp(top - m0), axis=0, keepdims=True)
                tau = top[PEER_TOPK - 1:PEER_TOPK, :]
                s1_t = s1[:, ln]
                acc_s[pl.ds(row0, PEER_NKEYS), ln] = (
                    (tau - s1_t) - PEER_GUARD * (jnp.abs(tau) + jnp.abs(s1_t)))
                sc_s[pl.ds(PEER_HEADS * PEER_NKEYS + row0, PEER_NKEYS), ln] = (
                    jnp.exp(s1[:, ln] - a_s[0:1, :]) / den)
                se_s[lt, hd, 0, 0:PEER_NKEYS, :] = s2[:, ln]
                se_s[lt, hd, 1, 0:PEER_NKEYS, :] = jnp.exp(s2[:, ln] - b_s[0:1, :])
            return carry

        lax.fori_loop(0, PEER_HEADS, head_body, 0)

        for hd in range(PEER_HEADS):
            rows = pl.ds(hd, PEER_NKEYS, stride=PEER_HEADS)
            for lt in range(n_lt):
                ln = slice(lt * LANES, (lt + 1) * LANES)
                thr_s[lt, rows, :] = acc_s[hd * PEER_NKEYS:(hd + 1) * PEER_NKEYS, ln]
                e1r_s[lt, rows, :] = sc_s[(PEER_HEADS + hd) * PEER_NKEYS:
                                          (PEER_HEADS + hd + 1) * PEER_NKEYS, ln]
        acc_s[...] = jnp.zeros_like(acc_s)

    key0 = [sum(PEER_PART_KEYS[:p]) for p in range(len(PEER_PART_KEYS) + 1)]
    part_rows = [slice(key0[p] * PEER_NKEYS, key0[p + 1] * PEER_NKEYS)
                 for p in range(len(PEER_PART_KEYS))]

    def pre_dot(part):
        rows = part_rows[part]
        sc_s[rows, :] = jnp.dot(u_ref[rows, :], ht_s[...], preferred_element_type=F32)

    chunk_rows = pl.ds(pl.multiple_of(c * (PEER_CHUNK_KEYS * PEER_HEADS), PEER_CHUNK_KEYS * PEER_HEADS),
                       PEER_CHUNK_KEYS * PEER_HEADS)
    for lt in range(n_lt):
        thr_c[lt] = thr_s[lt, chunk_rows, :]
        e1_c[lt] = e1r_s[lt, chunk_rows, :]

    def build(part):
        for j in range(key0[part], key0[part + 1]):
            rows = slice(j * PEER_NKEYS, (j + 1) * PEER_NKEYS)
            for lt in range(n_lt):
                ln = slice(lt * LANES, (lt + 1) * LANES)
                wgt = jnp.zeros((PEER_NKEYS, LANES), F32)
                thr_rows = thr_c[lt, j * PEER_HEADS:(j + 1) * PEER_HEADS, :]
                e1_rows = e1_c[lt, j * PEER_HEADS:(j + 1) * PEER_HEADS, :]
                for hd in range(PEER_HEADS):
                    s2 = se_s[lt, hd, 0, 0:PEER_NKEYS, :]
                    p = e1_rows[hd:hd + 1, :] * se_s[lt, hd, 1, 0:PEER_NKEYS, :]
                    wgt = wgt + jnp.where(s2 >= thr_rows[hd:hd + 1, :], p, 0.0)
                pre = sc_s[rows, ln]
                act = 0.5 * pre * (1.0 + lax.erf(pre * (1.0 / math.sqrt(2.0))))
                w_s[rows, ln] = (wgt * act).astype(BF16)

    def out_dot(part):
        rows = part_rows[part]
        acc_s[...] += jnp.dot(vt_ref[:, rows], w_s[rows, :], preferred_element_type=F32)

    n_parts = len(PEER_PART_KEYS)
    pre_dot(0)
    for part in range(n_parts):
        if part + 1 < n_parts:
            pre_dot(part + 1)
        build(part)
        if part > 0:
            out_dot(part - 1)
    out_dot(n_parts - 1)

    @pl.when(c == pl.num_programs(1) - 1)
    def _finish():
        o_ref[...] = x_ref[...] + acc_s[...].T


def _peer(x, g, wqt_bf16, sk_bf16, u_bf16, vt_bf16, tok=PEER_TOK):
    t, d = x.shape
    assert t % tok == 0 and sum(PEER_PART_KEYS) == PEER_CHUNK_KEYS
    assert PEER_HEADS * PEER_NKEYS <= min(d, PEER_CHUNK // 2)
    n_chunks = PEER_EXPERTS // PEER_CHUNK
    hq = PEER_HEADS * PEER_DK
    return pl.pallas_call(
        _peer_kernel,
        grid=(t // tok, n_chunks),
        in_specs=[
            pl.BlockSpec((tok, d), lambda i, c: (i, 0), pipeline_mode=pl.Buffered(1)),
            pl.BlockSpec((1, d), lambda i, c: (0, 0)),
            pl.BlockSpec((hq, d), lambda i, c: (0, 0), pipeline_mode=pl.Buffered(1)),
            pl.BlockSpec((2, PEER_NKEYS, PEER_DK // 2), lambda i, c: (0, 0, 0)),
            pl.BlockSpec((PEER_CHUNK, d), lambda i, c: (c, 0)),
            pl.BlockSpec((d, PEER_CHUNK), lambda i, c: (0, c)),
        ],
        out_specs=pl.BlockSpec((tok, d), lambda i, c: (i, 0)),
        out_shape=jax.ShapeDtypeStruct((t, d), F32),
        scratch_shapes=[
            pltpu.VMEM((d, tok), BF16),
            pltpu.VMEM((PEER_DK, tok), BF16),
            pltpu.VMEM((tok // LANES, PEER_HEADS, 2, PEER_NKEYS + SUBLANES, LANES), F32),
            pltpu.VMEM((tok // LANES, PEER_NKEYS * PEER_HEADS, LANES), F32),
            pltpu.VMEM((tok // LANES, PEER_NKEYS * PEER_HEADS, LANES), F32),
            pltpu.VMEM((tok // LANES, PEER_CHUNK_KEYS * PEER_HEADS, LANES), F32),
            pltpu.VMEM((tok // LANES, PEER_CHUNK_KEYS * PEER_HEADS, LANES), F32),
            pltpu.VMEM((PEER_TOPK, LANES), F32),
            pltpu.VMEM((PEER_TOPK, LANES), F32),
            pltpu.VMEM((PEER_NCAND, LANES), F32),
            pltpu.VMEM((PEER_TOPK, LANES), F32),
            pltpu.VMEM((PEER_CHUNK, tok), F32),
            pltpu.VMEM((PEER_CHUNK, tok), BF16),
            pltpu.VMEM((d, tok), F32),
        ],
        compiler_params=pltpu.CompilerParams(
            dimension_semantics=("parallel", "arbitrary"), vmem_limit_bytes=VMEM_LIMIT),
        name="peer",
    )(x, g.reshape(1, d), wqt_bf16, sk_bf16, u_bf16, vt_bf16)


def _pool_tile_plan(ct):
    lo = ct * LANES
    hi = lo + LANES - 1
    return [(POOL_WINDOWS[g], (g + 1) * POOL_GW) for g in range(lo // POOL_GW, hi // POOL_GW + 1)]


def _pool_diff_tile(load_shifted, z_tile, pos1, ct):
    plan = _pool_tile_plan(ct)
    wanted = {w for w, _ in plan}
    acc = z_tile
    snaps = {}
    for j in range(1, max(wanted)):
        acc = acc + load_shifted(j)
        if j + 1 in wanted:
            snaps[j + 1] = acc
    if len(plan) == 1:
        w = plan[0][0]
        win = snaps[w]
        cnt = jnp.minimum(float(w), pos1)
    else:
        (w_lo, edge), (w_hi, _) = plan
        lane = lax.broadcasted_iota(jnp.int32, (1,) * (z_tile.ndim - 1) + (LANES,), z_tile.ndim - 1)
        in_lo = lane + ct * LANES < edge
        win = jnp.where(in_lo, snaps[w_lo], snaps[w_hi])
        cnt = jnp.minimum(jnp.where(in_lo, float(w_lo), float(w_hi)), pos1)
    return win / cnt - z_tile


def _pool_prompt_kernel(z_ref, wbd_ref, scale_ref, mix_ref, state_ref, zc_s, d_s):
    i = pl.program_id(1)
    tile = z_ref.shape[0]

    @pl.when(i == 0)
    def _start():
        zc_s[0:POOL_HALO, :] = jnp.zeros((POOL_HALO, MIX_W), F32)

    zc_s[POOL_HALO:POOL_HALO + tile, :] = z_ref[...]
    pos1 = (i * tile + lax.broadcasted_iota(jnp.int32, (tile, 1), 0) + 1).astype(F32)
    for ct in range(MIX_W // LANES):
        ln = slice(ct * LANES, (ct + 1) * LANES)
        diff = _pool_diff_tile(
            lambda j: zc_s[POOL_HALO - j:POOL_HALO - j + tile, ln], z_ref[:, ln], pos1, ct)
        d_s[:, ln] = diff.astype(BF16)
    mix_ref[...] = jnp.dot(d_s[...], wbd_ref[...], preferred_element_type=F32) * scale_ref[...]
    last = z_ref[tile - POOL_HALO:tile, :]
    state_ref[...] = last
    zc_s[0:POOL_HALO, :] = last


def _pool_prompt(proj, nb, seq, wbd_bf16, scale, tile=TOK_TILE):
    nt = seq // tile
    assert seq % tile == 0 and tile >= POOL_HALO
    return pl.pallas_call(
        _pool_prompt_kernel,
        grid=(nb, nt),
        in_specs=[
            pl.BlockSpec((tile, MIX_W), lambda n, i: (n * nt + i, 0)),
            pl.BlockSpec((MIX_W, MIX_W), lambda n, i: (0, 0)),
            pl.BlockSpec((1, MIX_W), lambda n, i: (0, 0)),
        ],
        out_specs=[
            pl.BlockSpec((tile, MIX_W), lambda n, i: (n * nt + i, 0)),
            pl.BlockSpec((None, POOL_HALO, MIX_W), lambda n, i: (n, 0, 0)),
        ],
        out_shape=[
            jax.ShapeDtypeStruct((nb * seq, MIX_W), F32),
            jax.ShapeDtypeStruct((nb, POOL_HALO, MIX_W), F32),
        ],
        scratch_shapes=[
            pltpu.VMEM((POOL_HALO + tile, MIX_W), F32),
            pltpu.VMEM((tile, MIX_W), BF16),
        ],
        compiler_params=pltpu.CompilerParams(
            dimension_semantics=("arbitrary", "arbitrary"), vmem_limit_bytes=VMEM_LIMIT),
        name="pool_prompt",
    )(proj, wbd_bf16, scale.reshape(1, MIX_W))


def _pool_sample_kernel(z_ref, st_ref, wbd_ref, scale_ref, mix_ref, nst_ref, zc_s, d_s, *, pos0):
    grp, n_state, _ = st_ref.shape
    t = z_ref.shape[0] // grp
    zc_s[:, POOL_HALO - n_state:POOL_HALO, :] = st_ref[...]
    zc_s[:, POOL_HALO:POOL_HALO + t, :] = z_ref[...].reshape(grp, t, MIX_W)
    pos1 = (pos0 + lax.broadcasted_iota(jnp.int32, (1, t, 1), 1) + 1).astype(F32)
    for ct in range(MIX_W // LANES):
        ln = slice(ct * LANES, (ct + 1) * LANES)
        diff = _pool_diff_tile(
            lambda j: zc_s[:, POOL_HALO - j:POOL_HALO - j + t, ln],
            zc_s[:, POOL_HALO:POOL_HALO + t, ln], pos1, ct)
        d_s[:, ln] = diff.reshape(grp * t, LANES).astype(BF16)
    mix_ref[...] = jnp.dot(d_s[...], wbd_ref[...], preferred_element_type=F32) * scale_ref[...]
    nst_ref[...] = zc_s[:, POOL_HALO + t - n_state:POOL_HALO + t, :]


def _pool_sample(proj, nb, t, state, wbd_bf16, scale, pos0, grp=SAMPLE_GROUP):
    n_state = state.shape[1]
    assert nb % grp == 0 and t % SUBLANES == 0
    assert max(POOL_WINDOWS) - 1 <= n_state <= POOL_HALO - 1
    return pl.pallas_call(
        functools.partial(_pool_sample_kernel, pos0=pos0),
        grid=(nb // grp,),
        in_specs=[
            pl.BlockSpec((grp * t, MIX_W), lambda i: (i, 0)),
            pl.BlockSpec((grp, n_state, MIX_W), lambda i: (i, 0, 0)),
            pl.BlockSpec((MIX_W, MIX_W), lambda i: (0, 0)),
            pl.BlockSpec((1, MIX_W), lambda i: (0, 0)),
        ],
        out_specs=[
            pl.BlockSpec((grp * t, MIX_W), lambda i: (i, 0)),
            pl.BlockSpec((grp, n_state, MIX_W), lambda i: (i, 0, 0)),
        ],
        out_shape=[
            jax.ShapeDtypeStruct((nb * t, MIX_W), F32),
            jax.ShapeDtypeStruct((nb, n_state, MIX_W), F32),
        ],
        scratch_shapes=[
            pltpu.VMEM((grp, POOL_HALO + t, MIX_W), F32),
            pltpu.VMEM((grp * t, MIX_W), BF16),
        ],
        compiler_params=pltpu.CompilerParams(
            dimension_semantics=("parallel",), vmem_limit_bytes=VMEM_LIMIT),
        name="pool_sample",
    )(proj, state, wbd_bf16, scale.reshape(1, MIX_W))


def _mem_out_kernel(x_ref, mix_ref, qm_ref, mk_ref, mv_ref, wmix_ref, wmem_ref, o_ref):
    grp = mk_ref.shape[0]
    rows = x_ref.shape[0]
    tq = rows // grp
    acc = x_ref[...] + jnp.dot(mix_ref[...].astype(BF16), wmix_ref[...],
                               preferred_element_type=F32)
    for h in range(MEM_HEADS):
        cols = slice(h * MEM_HD, (h + 1) * MEM_HD)
        q = qm_ref[:, cols].astype(BF16).reshape(grp, tq, MEM_HD)
        k = mk_ref[:, :, cols].astype(BF16)
        v = mv_ref[:, :, cols].astype(BF16)
        s = jnp.einsum('bqd,bkd->bqk', q, k, preferred_element_type=F32) * (MEM_HD ** -0.5)
        e = jnp.exp(s - jnp.max(s, axis=-1, keepdims=True))
        p = e / jnp.sum(e, axis=-1, keepdims=True)
        o = jnp.einsum('bqk,bkd->bqd', p.astype(BF16), v, preferred_element_type=F32)
        acc = acc + jnp.dot(o.reshape(rows, MEM_HD).astype(BF16), wmem_ref[cols, :],
                            preferred_element_type=F32)
    o_ref[...] = acc


def _mem_out(x, mix, proj, qm_col_blk, mem_k, mem_v, wmix_bf16, wmem_bf16, grp, tq):
    rows, d = x.shape
    kmix = mix.shape[1]
    n_seq, n_mem, _ = mem_k.shape
    step = grp * tq
    per_seq = rows // n_seq
    assert rows % step == 0 and tq % SUBLANES == 0
    assert (grp == 1 and per_seq % tq == 0) or per_seq == tq
    tiles_per_seq = per_seq // tq
    seq_blk = (lambda i: i // tiles_per_seq) if grp == 1 else (lambda i: i)
    return pl.pallas_call(
        _mem_out_kernel,
        grid=(rows // step,),
        in_specs=[
            pl.BlockSpec((step, d), lambda i: (i, 0)),
            pl.BlockSpec((step, kmix), lambda i: (i, 0)),
            pl.BlockSpec((step, MEM_W), lambda i: (i, qm_col_blk)),
            pl.BlockSpec((grp, n_mem, MEM_W), lambda i: (seq_blk(i), 0, 0)),
            pl.BlockSpec((grp, n_mem, MEM_W), lambda i: (seq_blk(i), 0, 0)),
            pl.BlockSpec((kmix, d), lambda i: (0, 0)),
            pl.BlockSpec((MEM_W, d), lambda i: (0, 0)),
        ],
        out_specs=pl.BlockSpec((step, d), lambda i: (i, 0)),
        out_shape=jax.ShapeDtypeStruct((rows, d), F32),
        compiler_params=pltpu.CompilerParams(
            dimension_semantics=("parallel",), vmem_limit_bytes=VMEM_LIMIT),
        name="mem_out",
    )(x, mix, proj, mem_k, mem_v, wmix_bf16, wmem_bf16)


def _dil_prompt_kernel(slope_ref, q_ref, kc_ref, kp_ref, vc_ref, vp_ref, o_ref, og_s, lse_s):
    win = pl.program_id(1)
    rows = q_ref.shape[0]
    blk = DIL_BLOCK
    slope = slope_ref[...]
    ri = lax.broadcasted_iota(jnp.int32, (blk, blk), 0)
    ci = lax.broadcasted_iota(jnp.int32, (blk, blk), 1)
    diff = (ri - ci).astype(F32)
    own_ok = (ri >= ci)[None]
    prev_ok = (ci >= ri)[None]
    not_first = win > 0
    scale = B_HD ** -0.5

    for g, (w, d) in enumerate(DIL_PATTERNS):
        per_res = rows // (blk * d)
        assert w // d == blk and rows % (blk * d) == 0 and per_res & (per_res - 1) == 0

        def rows_of(r, j):
            return pl.ds(j * blk * d + r, blk, stride=d) if d > 1 else pl.ds(j * blk, blk)

        own = [rows_of(r, j) for r in range(d) for j in range(per_res)]
        prv = [(kc_ref, vc_ref, rows_of(r, j - 1)) if j > 0
               else (kp_ref, vp_ref, rows_of(r, per_res - 1))
               for r in range(d) for j in range(per_res)]
        q = jnp.stack([q_ref[o, :] for o in own]).astype(BF16)
        k_own = jnp.stack([kc_ref[o, :] for o in own]).astype(BF16)
        v_own = jnp.stack([vc_ref[o, :] for o in own]).astype(BF16)
        k_prev = jnp.stack([kr[o, :] for kr, _, o in prv]).astype(BF16)
        v_prev = jnp.stack([vr[o, :] for _, vr, o in prv]).astype(BF16)
        bias_own = (-(slope * float(d)) * diff)[None]
        bias_prev = (-(slope * float(d)) * (diff + float(blk)))[None]
        s_own = jnp.einsum('bqd,bkd->bqk', q, k_own, preferred_element_type=F32) * scale + bias_own
        s_own = jnp.where(own_ok, s_own, NEG)
        s_prev = jnp.einsum('bqd,bkd->bqk', q, k_prev, preferred_element_type=F32) * scale + bias_prev
        bi = lax.broadcasted_iota(jnp.int32, (len(own), blk, blk), 0)
        has_prev = jnp.logical_or((bi & (per_res - 1)) != 0, not_first)
        s_prev = jnp.where(jnp.logical_and(prev_ok, has_prev), s_prev, NEG)
        m = jnp.maximum(jnp.max(s_own, axis=-1, keepdims=True),
                        jnp.max(s_prev, axis=-1, keepdims=True))
        e_own = jnp.exp(s_own - m)
        e_prev = jnp.exp(s_prev - m)
        den = jnp.sum(e_own, axis=-1, keepdims=True) + jnp.sum(e_prev, axis=-1, keepdims=True)
        o = (jnp.einsum('bqk,bkd->bqd', (e_own / den).astype(BF16), v_own,
                        preferred_element_type=F32)
             + jnp.einsum('bqk,bkd->bqd', (e_prev / den).astype(BF16), v_prev,
                          preferred_element_type=F32))
        lse = jnp.broadcast_to(m + jnp.log(den), o.shape)
        for b, rows_b in enumerate(own):
            og_s[g, rows_b, :] = o[b]
            lse_s[g, rows_b, :] = lse[b]

    top = jnp.maximum(jnp.maximum(lse_s[0], lse_s[1]), lse_s[2])
    wts = [jnp.exp(lse_s[g] - top) for g in range(len(DIL_PATTERNS))]
    tot = wts[0] + wts[1] + wts[2]
    o_ref[...] = ((wts[0] / tot) * og_s[0] + (wts[1] / tot) * og_s[1] + (wts[2] / tot) * og_s[2])


def _dil_prompt(proj_pad, kv_pad, nb, seq, slopes_b):
    win = DIL_WIN
    nw = seq // win
    assert seq % win == 0 and len(DIL_PATTERNS) == 3
    cur = lambda col0: (lambda n, w, h: (n * nw + w, col0 + h))
    prev = lambda col0: (lambda n, w, h: (n * nw + jnp.maximum(w - 1, 0), col0 + h))
    return pl.pallas_call(
        _dil_prompt_kernel,
        grid=(nb, nw, B_HEADS),
        in_specs=[
            pl.BlockSpec((None, 1, LANES), lambda n, w, h: (h, 0, 0)),
            pl.BlockSpec((win, LANES), cur(0)),
            pl.BlockSpec((win, LANES), cur(0)),
            pl.BlockSpec((win, LANES), prev(0)),
            pl.BlockSpec((win, LANES), cur(B_HEADS)),
            pl.BlockSpec((win, LANES), prev(B_HEADS)),
        ],
        out_specs=pl.BlockSpec((win, LANES), cur(0)),
        out_shape=jax.ShapeDtypeStruct((nb * seq, B_HEADS * LANES), F32),
        scratch_shapes=[
            pltpu.VMEM((len(DIL_PATTERNS), win, LANES), F32),
            pltpu.VMEM((len(DIL_PATTERNS), win, LANES), F32),
        ],
        compiler_params=pltpu.CompilerParams(
            dimension_semantics=("parallel", "parallel", "parallel"),
            vmem_limit_bytes=VMEM_LIMIT),
        name="dil_prompt",
    )(slopes_b, proj_pad, kv_pad, kv_pad, kv_pad, kv_pad)


def _dil_sample_kernel(q_ref, kc_ref, vc_ref, kn_ref, vn_ref, bc_ref, bn_ref, hm_ref, o_ref,
                       knp_s, vnp_s, pc_s, *, row_lo):
    t = q_ref.shape[0]
    nt = (((1,), (1,)), ((), ()))
    scale = B_HD ** -0.5
    hm = hm_ref[...]
    qbd = (jnp.concatenate([q_ref[...]] * B_HEADS, axis=0) * hm).astype(BF16)
    knp_s[...] = jnp.zeros_like(knp_s)
    vnp_s[...] = jnp.zeros_like(vnp_s)
    knp_s[0:t, :] = kn_ref[...]
    vnp_s[0:t, :] = vn_ref[...]
    s_c = lax.dot_general(qbd, kc_ref[...].astype(BF16), nt, preferred_element_type=F32) * scale
    s_n = lax.dot_general(qbd, knp_s[...].astype(BF16), nt, preferred_element_type=F32) * scale
    stats = []
    for g in range(len(DIL_PATTERNS)):
        lo = row_lo[g]
        bc = bc_ref[g, :, lo:]
        bn = bn_ref[g]
        sc = jnp.where(bc > 0.5 * NEG, s_c[:, lo:] + bc, NEG)
        sn = jnp.where(bn > 0.5 * NEG, s_n + bn, NEG)
        m = jnp.maximum(jnp.max(sc, axis=-1, keepdims=True), jnp.max(sn, axis=-1, keepdims=True))
        ec = jnp.exp(sc - m)
        en = jnp.exp(sn - m)
        den = jnp.sum(ec, axis=-1, keepdims=True) + jnp.sum(en, axis=-1, keepdims=True)
        stats.append((ec / den, en / den, m + jnp.log(den)))
    top = jnp.maximum(jnp.maximum(stats[0][2], stats[1][2]), stats[2][2])
    wts = [jnp.exp(st[2] - top) for st in stats]
    tot = wts[0] + wts[1] + wts[2]
    pc_s[...] = jnp.zeros_like(pc_s)
    pn = jnp.zeros((B_HEADS * t, knp_s.shape[0]), F32)
    for g in range(len(DIL_PATTERNS)):
        lo = row_lo[g]
        alpha = wts[g] / tot
        pc_s[:, lo:] += alpha * stats[g][0]
        pn = pn + alpha * stats[g][1]
    o = (jnp.dot(pc_s[...].astype(BF16), vc_ref[...].astype(BF16), preferred_element_type=F32)
         + jnp.dot(pn.astype(BF16), vnp_s[...].astype(BF16), preferred_element_type=F32))
    o_ref[...] = jnp.sum((o * hm).reshape(B_HEADS, t, MIX_W), axis=0)


def _dil_sample_tables(past, t):
    slopes = np.asarray([2.0 ** (-8.0 * (h + 1) / B_HEADS) for h in range(B_HEADS)], np.float32)
    qpos = past + np.arange(t)
    col = np.arange(past + LANES)
    exists = col < past + t
    delta = qpos[:, None] - col[None, :]
    bias = np.full((len(DIL_PATTERNS), B_HEADS, t, past + LANES), NEG, np.float32)
    col_lo = []
    for g, (w, d) in enumerate(DIL_PATTERNS):
        ok = (delta >= 0) & (delta <= w) & (delta % d == 0) & exists[None, :]
        vals = -slopes[:, None, None] * delta[None].astype(np.float32)
        bias[g] = np.where(ok[None], vals, np.float32(NEG))
        col_lo.append(max(0, (past - w) // LANES * LANES))
    bias = bias.reshape(len(DIL_PATTERNS), B_HEADS * t, past + LANES)
    hm = (np.arange(MIX_W)[None, :] // B_HD == np.arange(B_HEADS * t)[:, None] // t)
    return bias[:, :, :past], bias[:, :, past:], hm.astype(np.float32), tuple(col_lo)


def _dil_sample(proj, kv_new, cache_k, cache_v):
    nb, past, _ = cache_k.shape
    t = proj.shape[0] // nb
    assert t % SUBLANES == 0 and t <= LANES and past % LANES == 0 and len(DIL_PATTERNS) == 3
    bias_c, bias_n, hm, col_lo = _dil_sample_tables(past, t)
    ng = len(DIL_PATTERNS)
    return pl.pallas_call(
        functools.partial(_dil_sample_kernel, row_lo=col_lo),
        grid=(nb,),
        in_specs=[
            pl.BlockSpec((t, MIX_W), lambda b: (b, 0)),
            pl.BlockSpec((None, past, MIX_W), lambda b: (b, 0, 0)),
            pl.BlockSpec((None, past, MIX_W), lambda b: (b, 0, 0)),
            pl.BlockSpec((t, MIX_W), lambda b: (b, 0)),
            pl.BlockSpec((t, MIX_W), lambda b: (b, 1)),
            pl.BlockSpec((ng, B_HEADS * t, past), lambda b: (0, 0, 0)),
            pl.BlockSpec((ng, B_HEADS * t, LANES), lambda b: (0, 0, 0)),
            pl.BlockSpec((B_HEADS * t, MIX_W), lambda b: (0, 0)),
        ],
        out_specs=pl.BlockSpec((t, MIX_W), lambda b: (b, 0)),
        out_shape=jax.ShapeDtypeStruct((nb * t, MIX_W), F32),
        scratch_shapes=[
            pltpu.VMEM((LANES, MIX_W), F32),
            pltpu.VMEM((LANES, MIX_W), F32),
            pltpu.VMEM((B_HEADS * t, past), F32),
        ],
        compiler_params=pltpu.CompilerParams(
            dimension_semantics=("parallel",), vmem_limit_bytes=VMEM_LIMIT),
        name="dil_sample",
    )(proj, cache_k, cache_v, kv_new, kv_new, jnp.asarray(bias_c), jnp.asarray(bias_n),
      jnp.asarray(hm))


def _pad_head_cols(w):
    d = w.shape[0]
    w = jnp.pad(w.reshape(d, B_HEADS, B_HD), ((0, 0), (0, 0), (0, LANES - B_HD)))
    return w.reshape(d, B_HEADS * LANES)


def _unpad_heads(a):
    return a.reshape(a.shape[:-1] + (B_HEADS, LANES))[..., :B_HD]


def kernel(x_prompt, x_sample, state_pool, cache_win_k, cache_win_v, cache_mem_k, cache_mem_v,
           mem_prompt, norm_mix, w_in, pool_w, pool_scale, norm_mem, w_mem_kv, w_out,
           norm_kv, w_kv, norm_ffn, peer_wq, peer_subkeys, peer_u, peer_v, norm_final):
    nb_p, seq, d = x_prompt.shape
    nb_s, dec_seq, _ = x_sample.shape
    n_mem = mem_prompt.shape[1]
    past = cache_win_k.shape[1]
    tp = nb_p * seq
    ts = nb_s * dec_seq

    mem_flat = mem_prompt.reshape(nb_p * n_mem, d)
    mkv = [_norm_matmul(mem_flat, norm_mem[l], w_mem_kv[l].astype(BF16)) for l in range(DEPTH)]
    mem_k_p = jnp.stack([m[:, :MEM_W].reshape(nb_p, n_mem, MEM_W) for m in mkv], axis=0)
    mem_v_p = jnp.stack([m[:, MEM_W:].reshape(nb_p, n_mem, MEM_W) for m in mkv], axis=0)
    mem_k_s = cache_mem_k.reshape(DEPTH, nb_s, n_mem, MEM_W)
    mem_v_s = cache_mem_v.reshape(DEPTH, nb_s, n_mem, MEM_W)

    slopes_b = jnp.broadcast_to(
        jnp.asarray([2.0 ** (-8.0 * (h + 1) / B_HEADS) for h in range(B_HEADS)], F32)[:, None, None],
        (B_HEADS, 1, LANES))
    cache_k = cache_win_k.reshape(nb_s, past, MIX_W)
    cache_v = cache_win_v.reshape(nb_s, past, MIX_W)

    xp = x_prompt.reshape(tp, d)
    xs = x_sample.reshape(ts, d)
    pool_p, pool_s = [], []
    kv_p = kv_s = None
    for l in range(DEPTH):
        w_in_l = w_in[l].astype(BF16)
        w_out_l = w_out[l].astype(BF16)
        proj_s = _norm_matmul(xs, norm_mix[l], w_in_l)
        if l < N_A:
            proj_p = _norm_matmul(xp, norm_mix[l], w_in_l)
            wbd = jax.scipy.linalg.block_diag(*[pool_w[l, g] for g in range(POOL_GROUPS)]).astype(BF16)
            mix_p, last_p = _pool_prompt(proj_p, nb_p, seq, wbd, pool_scale[l])
            pool_p.append(last_p[:, POOL_HALO - POOL_STATE:])
            mix_s, new_state = _pool_sample(proj_s, nb_s, dec_seq, state_pool[l], wbd,
                                            pool_scale[l], PAST_LEN)
            pool_s.append(new_state)
            wmix_p = w_out_l[:MIX_W]
            qm_blk_p = MIX_W // MEM_W
        else:
            w_in_pad = jnp.concatenate([_pad_head_cols(w_in[l][:, :MIX_W]), w_in[l][:, MIX_W:]],
                                       axis=1).astype(BF16)
            proj_p = _norm_matmul(xp, norm_mix[l], w_in_pad)
            mix_p = _dil_prompt(proj_p, kv_p, nb_p, seq, slopes_b)
            mix_s = _dil_sample(proj_s, kv_s, cache_k, cache_v)
            wmix_p = _pad_head_cols(w_out[l][:MIX_W].T).T.astype(BF16)
            qm_blk_p = B_HEADS * LANES // MEM_W
        xp = _mem_out(xp, mix_p, proj_p, qm_blk_p, mem_k_p[l], mem_v_p[l], wmix_p,
                      w_out_l[MIX_W:], 1, TOK_TILE)
        xs = _mem_out(xs, mix_s, proj_s, MIX_W // MEM_W, mem_k_s[l], mem_v_s[l], w_out_l[:MIX_W],
                      w_out_l[MIX_W:], SAMPLE_GROUP, dec_seq)
        peer_w = (norm_ffn[l], peer_wq[l].T.astype(BF16), peer_subkeys[l].astype(BF16),
                  peer_u[l].astype(BF16), peer_v[l].T.astype(BF16))
        xp = _peer(xp, *peer_w)
        xs = _peer(xs, *peer_w)
        if l == N_A - 1:
            w_kv_pad = jnp.concatenate([_pad_head_cols(w_kv[:, :MIX_W]),
                                        _pad_head_cols(w_kv[:, MIX_W:])], axis=1).astype(BF16)
            kv_p = _norm_matmul(xp, norm_kv, w_kv_pad)
            kv_s = _norm_matmul(xs, norm_kv, w_kv.astype(BF16))
    y_p = _final_norm(xp, norm_final)
    y_s = _final_norm(xs, norm_final)
    keep = min(WINDOW_MAX, seq)
    kv_p4 = kv_p.reshape(nb_p, seq, 2 * B_HEADS * LANES)[:, seq - keep:]
    return (y_p.reshape(nb_p, seq, d), y_s.reshape(nb_s, dec_seq, d),
            jnp.stack(pool_p, axis=0),
            _unpad_heads(kv_p4[..., :B_HEADS * LANES]), _unpad_heads(kv_p4[..., B_HEADS * LANES:]),
            mem_k_p.reshape(DEPTH, nb_p, n_mem, MEM_HEADS, MEM_HD),
            mem_v_p.reshape(DEPTH, nb_p, n_mem, MEM_HEADS, MEM_HD),
            jnp.stack(pool_s, axis=0),
            kv_s[:, :MIX_W].reshape(nb_s, dec_seq, B_HEADS, B_HD),
            kv_s[:, MIX_W:].reshape(nb_s, dec_seq, B_HEADS, B_HD))
```

```python
import functools
import math

import numpy as np
import jax
import jax.numpy as jnp
from jax import lax
from jax.experimental import pallas as pl
from jax.experimental.pallas import tpu as pltpu

F32 = jnp.float32
BF16 = jnp.bfloat16

D_MODEL = 1024
DEPTH = 4
N_A = DEPTH // 2
MIX_W = 3 * D_MODEL // 4
MEM_HEADS = 4
MEM_HD = (D_MODEL - MIX_W) // MEM_HEADS
MEM_W = MEM_HEADS * MEM_HD
POOL_WINDOWS = (2, 4, 8, 16)
POOL_GROUPS = len(POOL_WINDOWS)
POOL_GW = MIX_W // POOL_GROUPS
POOL_STATE = max(POOL_WINDOWS) - 1
B_HEADS = 8
B_HD = MIX_W // B_HEADS
DIL_PATTERNS = ((128, 1), (512, 4), (2048, 16))
WINDOW_MAX = max(w for w, _ in DIL_PATTERNS)
PEER_HEADS = 8
PEER_NKEYS = 128
PEER_EXPERTS = PEER_NKEYS * PEER_NKEYS
PEER_DK = 256
PEER_TOPK = 16
PAST_LEN = 2048
EPS = 1e-6
NEG = -1e30

LANES = 128
SUBLANES = 8
TOK_TILE = 512
PEER_TOK = 512
PEER_CHUNK_KEYS = 16
PEER_CHUNK = PEER_CHUNK_KEYS * PEER_NKEYS
PEER_PART_KEYS = (4, 4, 4, 4)
PEER_STAIR = tuple(PEER_TOPK // (k + 1) for k in range(PEER_TOPK))
PEER_NCAND = -(-sum(PEER_STAIR) // SUBLANES) * SUBLANES
PEER_GUARD = 2.0 ** -21
POOL_HALO = 16
DIL_BLOCK = 128
DIL_WIN = 2048
SAMPLE_GROUP = 16
VMEM_LIMIT = 56 * 1024 * 1024


def _rms(x, g):
    r = lax.rsqrt(jnp.mean(x * x, axis=-1, keepdims=True) + EPS)
    return (x * r) * g


def _norm_matmul_kernel(x_ref, g_ref, w_ref, o_ref):
    h = _rms(x_ref[...], g_ref[...])
    o_ref[...] = jnp.dot(h.astype(BF16), w_ref[...], preferred_element_type=F32)


def _norm_matmul(x, g, w_bf16, tile=TOK_TILE):
    t, d = x.shape
    n = w_bf16.shape[1]
    tile = min(tile, t)
    assert t % tile == 0
    return pl.pallas_call(
        _norm_matmul_kernel,
        grid=(t // tile,),
        in_specs=[
            pl.BlockSpec((tile, d), lambda i: (i, 0)),
            pl.BlockSpec((1, d), lambda i: (0, 0)),
            pl.BlockSpec((d, n), lambda i: (0, 0)),
        ],
        out_specs=pl.BlockSpec((tile, n), lambda i: (i, 0)),
        out_shape=jax.ShapeDtypeStruct((t, n), F32),
        compiler_params=pltpu.CompilerParams(
            dimension_semantics=("parallel",), vmem_limit_bytes=VMEM_LIMIT),
        name="norm_matmul",
    )(x, g.reshape(1, d), w_bf16)


def _final_norm_kernel(x_ref, g_ref, o_ref):
    o_ref[...] = _rms(x_ref[...], g_ref[...])


def _final_norm(x, g, tile=TOK_TILE):
    t, d = x.shape
    assert t % tile == 0
    return pl.pallas_call(
        _final_norm_kernel,
        grid=(t // tile,),
        in_specs=[
            pl.BlockSpec((tile, d), lambda i: (i, 0)),
            pl.BlockSpec((1, d), lambda i: (0, 0)),
        ],
        out_specs=pl.BlockSpec((tile, d), lambda i: (i, 0)),
        out_shape=jax.ShapeDtypeStruct((t, d), F32),
        compiler_params=pltpu.CompilerParams(dimension_semantics=("parallel",)),
        name="final_norm",
    )(x, g.reshape(1, d))


def _extract_top(s, out_ref, n):
    for k in range(n):
        m = jnp.max(s, axis=0, keepdims=True)
        out_ref[k:k + 1, :] = m
        s = jnp.where(s == m, -jnp.inf, s)


def _odd_even_merge_sort(n):
    pairs = []
    p = 1
    while p < n:
        k = p
        while k >= 1:
            for j in range(k % p, n - k, 2 * k):
                for i in range(min(k, n - j - k)):
                    if (i + j) // (2 * p) == (i + j + k) // (2 * p):
                        pairs.append((i + j, i + j + k))
            k //= 2
        p *= 2
    return pairs


def _sorted_top(s, out_ref):
    n = s.shape[0] // SUBLANES
    assert s.shape[0] == n * SUBLANES and n & (n - 1) == 0

    def exchange(v, i, j):
        v[i], v[j] = jnp.maximum(v[i], v[j]), jnp.minimum(v[i], v[j])

    v = [s[i * SUBLANES:(i + 1) * SUBLANES, :] for i in range(n)]
    for i, j in _odd_even_merge_sort(n):
        exchange(v, i, j)
    shift = SUBLANES // 2
    while shift >= 1:
        other = [pltpu.roll(t, shift, 0) for t in v]
        v = [jnp.maximum(v[i], other[n - 1 - i]) for i in range(n)]
        stride = n // 2
        while stride >= 1:
            for i in range(n):
                if i & stride == 0:
                    exchange(v, i, i + stride)
            stride //= 2
        shift //= 2
    for k in range(n):
        out_ref[k:k + 1, :] = v[k][0:1, :]


def _peer_kernel(x_ref, g_ref, wqt_ref, sk_ref, u_ref, vt_ref, o_ref,
                 ht_s, q_s, se_s, thr_s, e1r_s, a_s, b_s, cand_s, top_s,
                 sc_s, w_s, acc_s):
    c = pl.program_id(1)
    tok = x_ref.shape[0]
    n_lt = tok // LANES

    @pl.when(c == 0)
    def _route():
        h = _rms(x_ref[...], g_ref[...])
        ht_s[...] = h.T.astype(BF16)
        cand_s[...] = jnp.full(cand_s.shape, -jnp.inf, F32)

        def head_body(hd, carry):
            base = pl.multiple_of(hd * PEER_DK, PEER_DK)
            half = PEER_DK // 2
            row0 = pl.multiple_of(hd * PEER_NKEYS, PEER_NKEYS)
            q_s[...] = jnp.dot(wqt_ref[pl.ds(base, PEER_DK), :], ht_s[...],
                               preferred_element_type=F32).astype(BF16)
            s1 = jnp.dot(sk_ref[0], q_s[0:half, :], preferred_element_type=F32)
            s2 = jnp.dot(sk_ref[1], q_s[half:PEER_DK, :], preferred_element_type=F32)
            for lt in range(n_lt):
                ln = slice(lt * LANES, (lt + 1) * LANES)
                _sorted_top(s1[:, ln], a_s)
                _sorted_top(s2[:, ln], b_s)
                off = 0
                for k, width in enumerate(PEER_STAIR):
                    cand_s[off:off + width, :] = a_s[k:k + 1, :] + b_s[0:width, :]
                    off += width
                _extract_top(cand_s[...], top_s, PEER_TOPK)
                top = top_s[...]
                m0 = top[0:1, :]
                den = jnp.sum(jnp.exp(top - m0), axis=0, keepdims=True)
                tau = top[PEER_TOPK - 1:PEER_TOPK, :]
                s1_t = s1[:, ln]
                acc_s[pl.ds(row0, PEER_NKEYS), ln] = (
                    (tau - s1_t) - PEER_GUARD * (jnp.abs(tau) + jnp.abs(s1_t)))
                sc_s[pl.ds(PEER_HEADS * PEER_NKEYS + row0, PEER_NKEYS), ln] = (
                    jnp.exp(s1[:, ln] - a_s[0:1, :]) / den)
                se_s[lt, hd, 0, 0:PEER_NKEYS, :] = s2[:, ln]
                se_s[lt, hd, 1, 0:PEER_NKEYS, :] = jnp.exp(s2[:, ln] - b_s[0:1, :])
            return carry

        lax.fori_loop(0, PEER_HEADS, head_body, 0)

        for hd in range(PEER_HEADS):
            rows = pl.ds(hd, PEER_NKEYS, stride=PEER_HEADS)
            for lt in range(n_lt):
                ln = slice(lt * LANES, (lt + 1) * LANES)
                thr_s[lt, rows, :] = acc_s[hd * PEER_NKEYS:(hd + 1) * PEER_NKEYS, ln]
                e1r_s[lt, rows, :] = sc_s[(PEER_HEADS + hd) * PEER_NKEYS:
                                          (PEER_HEADS + hd + 1) * PEER_NKEYS, ln]
        acc_s[...] = jnp.zeros_like(acc_s)

    key0 = [sum(PEER_PART_KEYS[:p]) for p in range(len(PEER_PART_KEYS) + 1)]
    part_rows = [slice(key0[p] * PEER_NKEYS, key0[p + 1] * PEER_NKEYS)
                 for p in range(len(PEER_PART_KEYS))]

    def pre_dot(part):
        rows = part_rows[part]
        sc_s[rows, :] = jnp.dot(u_ref[rows, :], ht_s[...], preferred_element_type=F32)

    def build(part):
        for j in range(key0[part], key0[part + 1]):
            r0 = pl.multiple_of((c * PEER_CHUNK_KEYS + j) * PEER_HEADS, PEER_HEADS)
            rows = slice(j * PEER_NKEYS, (j + 1) * PEER_NKEYS)
            for lt in range(n_lt):
                ln = slice(lt * LANES, (lt + 1) * LANES)
                wgt = jnp.zeros((PEER_NKEYS, LANES), F32)
                thr_rows = thr_s[lt, pl.ds(r0, PEER_HEADS), :]
                e1_rows = e1r_s[lt, pl.ds(r0, PEER_HEADS), :]
                for hd in range(PEER_HEADS):
                    s2 = se_s[lt, hd, 0, 0:PEER_NKEYS, :]
                    p = e1_rows[hd:hd + 1, :] * se_s[lt, hd, 1, 0:PEER_NKEYS, :]
                    wgt = wgt + jnp.where(s2 >= thr_rows[hd:hd + 1, :], p, 0.0)
                pre = sc_s[rows, ln]
                act = 0.5 * pre * (1.0 + lax.erf(pre * (1.0 / math.sqrt(2.0))))
                w_s[rows, ln] = (wgt * act).astype(BF16)

    def out_dot(part):
        rows = part_rows[part]
        acc_s[...] += jnp.dot(vt_ref[:, rows], w_s[rows, :], preferred_element_type=F32)

    n_parts = len(PEER_PART_KEYS)
    pre_dot(0)
    for part in range(n_parts):
        if part + 1 < n_parts:
            pre_dot(part + 1)
        build(part)
        if part > 0:
            out_dot(part - 1)
    out_dot(n_parts - 1)

    @pl.when(c == pl.num_programs(1) - 1)
    def _finish():
        o_ref[...] = x_ref[...] + acc_s[...].T


def _peer(x, g, wqt_bf16, sk_bf16, u_bf16, vt_bf16, tok=PEER_TOK):
    t, d = x.shape
    assert t % tok == 0 and sum(PEER_PART_KEYS) == PEER_CHUNK_KEYS
    assert PEER_HEADS * PEER_NKEYS <= min(d, PEER_CHUNK // 2)
    n_chunks = PEER_EXPERTS // PEER_CHUNK
    hq = PEER_HEADS * PEER_DK
    return pl.pallas_call(
        _peer_kernel,
        grid=(t // tok, n_chunks),
        in_specs=[
            pl.BlockSpec((tok, d), lambda i, c: (i, 0), pipeline_mode=pl.Buffered(1)),
            pl.BlockSpec((1, d), lambda i, c: (0, 0)),
            pl.BlockSpec((hq, d), lambda i, c: (0, 0), pipeline_mode=pl.Buffered(1)),
            pl.BlockSpec((2, PEER_NKEYS, PEER_DK // 2), lambda i, c: (0, 0, 0)),
            pl.BlockSpec((PEER_CHUNK, d), lambda i, c: (c, 0)),
            pl.BlockSpec((d, PEER_CHUNK), lambda i, c: (0, c)),
        ],
        out_specs=pl.BlockSpec((tok, d), lambda i, c: (i, 0)),
        out_shape=jax.ShapeDtypeStruct((t, d), F32),
        scratch_shapes=[
            pltpu.VMEM((d, tok), BF16),
            pltpu.VMEM((PEER_DK, tok), BF16),
            pltpu.VMEM((tok // LANES, PEER_HEADS, 2, PEER_NKEYS + SUBLANES, LANES), F32),
            pltpu.VMEM((tok // LANES, PEER_NKEYS * PEER_HEADS, LANES), F32),
            pltpu.VMEM((tok // LANES, PEER_NKEYS * PEER_HEADS, LANES), F32),
            pltpu.VMEM((PEER_TOPK, LANES), F32),
            pltpu.VMEM((PEER_TOPK, LANES), F32),
            pltpu.VMEM((PEER_NCAND, LANES), F32),
            pltpu.VMEM((PEER_TOPK, LANES), F32),
            pltpu.VMEM((PEER_CHUNK, tok), F32),
            pltpu.VMEM((PEER_CHUNK, tok), BF16),
            pltpu.VMEM((d, tok), F32),
        ],
        compiler_params=pltpu.CompilerParams(
            dimension_semantics=("parallel", "arbitrary"), vmem_limit_bytes=VMEM_LIMIT),
        name="peer",
    )(x, g.reshape(1, d), wqt_bf16, sk_bf16, u_bf16, vt_bf16)


def _pool_tile_plan(ct):
    lo = ct * LANES
    hi = lo + LANES - 1
    return [(POOL_WINDOWS[g], (g + 1) * POOL_GW) for g in range(lo // POOL_GW, hi // POOL_GW + 1)]


def _pool_diff_tile(load_shifted, z_tile, pos1, ct):
    plan = _pool_tile_plan(ct)
    wanted = {w for w, _ in plan}
    acc = z_tile
    snaps = {}
    for j in range(1, max(wanted)):
        acc = acc + load_shifted(j)
        if j + 1 in wanted:
            snaps[j + 1] = acc
    if len(plan) == 1:
        w = plan[0][0]
        win = snaps[w]
        cnt = jnp.minimum(float(w), pos1)
    else:
        (w_lo, edge), (w_hi, _) = plan
        lane = lax.broadcasted_iota(jnp.int32, (1,) * (z_tile.ndim - 1) + (LANES,), z_tile.ndim - 1)
        in_lo = lane + ct * LANES < edge
        win = jnp.where(in_lo, snaps[w_lo], snaps[w_hi])
        cnt = jnp.minimum(jnp.where(in_lo, float(w_lo), float(w_hi)), pos1)
    return win / cnt - z_tile


def _pool_prompt_kernel(z_ref, wbd_ref, scale_ref, mix_ref, state_ref, zc_s, d_s):
    i = pl.program_id(1)
    tile = z_ref.shape[0]

    @pl.when(i == 0)
    def _start():
        zc_s[0:POOL_HALO, :] = jnp.zeros((POOL_HALO, MIX_W), F32)

    zc_s[POOL_HALO:POOL_HALO + tile, :] = z_ref[...]
    pos1 = (i * tile + lax.broadcasted_iota(jnp.int32, (tile, 1), 0) + 1).astype(F32)
    for ct in range(MIX_W // LANES):
        ln = slice(ct * LANES, (ct + 1) * LANES)
        diff = _pool_diff_tile(
            lambda j: zc_s[POOL_HALO - j:POOL_HALO - j + tile, ln], z_ref[:, ln], pos1, ct)
        d_s[:, ln] = diff.astype(BF16)
    mix_ref[...] = jnp.dot(d_s[...], wbd_ref[...], preferred_element_type=F32) * scale_ref[...]
    last = z_ref[tile - POOL_HALO:tile, :]
    state_ref[...] = last
    zc_s[0:POOL_HALO, :] = last


def _pool_prompt(proj, nb, seq, wbd_bf16, scale, tile=TOK_TILE):
    nt = seq // tile
    assert seq % tile == 0 and tile >= POOL_HALO
    return pl.pallas_call(
        _pool_prompt_kernel,
        grid=(nb, nt),
        in_specs=[
            pl.BlockSpec((tile, MIX_W), lambda n, i: (n * nt + i, 0)),
            pl.BlockSpec((MIX_W, MIX_W), lambda n, i: (0, 0)),
            pl.BlockSpec((1, MIX_W), lambda n, i: (0, 0)),
        ],
        out_specs=[
            pl.BlockSpec((tile, MIX_W), lambda n, i: (n * nt + i, 0)),
            pl.BlockSpec((None, POOL_HALO, MIX_W), lambda n, i: (n, 0, 0)),
        ],
        out_shape=[
            jax.ShapeDtypeStruct((nb * seq, MIX_W), F32),
            jax.ShapeDtypeStruct((nb, POOL_HALO, MIX_W), F32),
        ],
        scratch_shapes=[
            pltpu.VMEM((POOL_HALO + tile, MIX_W), F32),
            pltpu.VMEM((tile, MIX_W), BF16),
        ],
        compiler_params=pltpu.CompilerParams(
            dimension_semantics=("arbitrary", "arbitrary"), vmem_limit_bytes=VMEM_LIMIT),
        name="pool_prompt",
    )(proj, wbd_bf16, scale.reshape(1, MIX_W))


def _pool_sample_kernel(z_ref, st_ref, wbd_ref, scale_ref, mix_ref, nst_ref, zc_s, d_s, *, pos0):
    grp, n_state, _ = st_ref.shape
    t = z_ref.shape[0] // grp
    zc_s[:, POOL_HALO - n_state:POOL_HALO, :] = st_ref[...]
    zc_s[:, POOL_HALO:POOL_HALO + t, :] = z_ref[...].reshape(grp, t, MIX_W)
    pos1 = (pos0 + lax.broadcasted_iota(jnp.int32, (1, t, 1), 1) + 1).astype(F32)
    for ct in range(MIX_W // LANES):
        ln = slice(ct * LANES, (ct + 1) * LANES)
        diff = _pool_diff_tile(
            lambda j: zc_s[:, POOL_HALO - j:POOL_HALO - j + t, ln],
            zc_s[:, POOL_HALO:POOL_HALO + t, ln], pos1, ct)
        d_s[:, ln] = diff.reshape(grp * t, LANES).astype(BF16)
    mix_ref[...] = jnp.dot(d_s[...], wbd_ref[...], preferred_element_type=F32) * scale_ref[...]
    nst_ref[...] = zc_s[:, POOL_HALO + t - n_state:POOL_HALO + t, :]


def _pool_sample(proj, nb, t, state, wbd_bf16, scale, pos0, grp=SAMPLE_GROUP):
    n_state = state.shape[1]
    assert nb % grp == 0 and t % SUBLANES == 0
    assert max(POOL_WINDOWS) - 1 <= n_state <= POOL_HALO - 1
    return pl.pallas_call(
        functools.partial(_pool_sample_kernel, pos0=pos0),
        grid=(nb // grp,),
        in_specs=[
            pl.BlockSpec((grp * t, MIX_W), lambda i: (i, 0)),
            pl.BlockSpec((grp, n_state, MIX_W), lambda i: (i, 0, 0)),
            pl.BlockSpec((MIX_W, MIX_W), lambda i: (0, 0)),
            pl.BlockSpec((1, MIX_W), lambda i: (0, 0)),
        ],
        out_specs=[
            pl.BlockSpec((grp * t, MIX_W), lambda i: (i, 0)),
            pl.BlockSpec((grp, n_state, MIX_W), lambda i: (i, 0, 0)),
        ],
        out_shape=[
            jax.ShapeDtypeStruct((nb * t, MIX_W), F32),
            jax.ShapeDtypeStruct((nb, n_state, MIX_W), F32),
        ],
        scratch_shapes=[
            pltpu.VMEM((grp, POOL_HALO + t, MIX_W), F32),
            pltpu.VMEM((grp * t, MIX_W), BF16),
        ],
        compiler_params=pltpu.CompilerParams(
            dimension_semantics=("parallel",), vmem_limit_bytes=VMEM_LIMIT),
        name="pool_sample",
    )(proj, state, wbd_bf16, scale.reshape(1, MIX_W))


def _mem_out_kernel(x_ref, mix_ref, qm_ref, mk_ref, mv_ref, wmix_ref, wmem_ref, o_ref):
    grp = mk_ref.shape[0]
    rows = x_ref.shape[0]
    tq = rows // grp
    acc = x_ref[...] + jnp.dot(mix_ref[...].astype(BF16), wmix_ref[...],
                               preferred_element_type=F32)
    for h in range(MEM_HEADS):
        cols = slice(h * MEM_HD, (h + 1) * MEM_HD)
        q = qm_ref[:, cols].astype(BF16).reshape(grp, tq, MEM_HD)
        k = mk_ref[:, :, cols].astype(BF16)
        v = mv_ref[:, :, cols].astype(BF16)
        s = jnp.einsum('bqd,bkd->bqk', q, k, preferred_element_type=F32) * (MEM_HD ** -0.5)
        e = jnp.exp(s - jnp.max(s, axis=-1, keepdims=True))
        p = e / jnp.sum(e, axis=-1, keepdims=True)
        o = jnp.einsum('bqk,bkd->bqd', p.astype(BF16), v, preferred_element_type=F32)
        acc = acc + jnp.dot(o.reshape(rows, MEM_HD).astype(BF16), wmem_ref[cols, :],
                            preferred_element_type=F32)
    o_ref[...] = acc


def _mem_out(x, mix, proj, qm_col_blk, mem_k, mem_v, wmix_bf16, wmem_bf16, grp, tq):
    rows, d = x.shape
    kmix = mix.shape[1]
    n_seq, n_mem, _ = mem_k.shape
    step = grp * tq
    per_seq = rows // n_seq
    assert rows % step == 0 and tq % SUBLANES == 0
    assert (grp == 1 and per_seq % tq == 0) or per_seq == tq
    tiles_per_seq = per_seq // tq
    seq_blk = (lambda i: i // tiles_per_seq) if grp == 1 else (lambda i: i)
    return pl.pallas_call(
        _mem_out_kernel,
        grid=(rows // step,),
        in_specs=[
            pl.BlockSpec((step, d), lambda i: (i, 0)),
            pl.BlockSpec((step, kmix), lambda i: (i, 0)),
            pl.BlockSpec((step, MEM_W), lambda i: (i, qm_col_blk)),
            pl.BlockSpec((grp, n_mem, MEM_W), lambda i: (seq_blk(i), 0, 0)),
            pl.BlockSpec((grp, n_mem, MEM_W), lambda i: (seq_blk(i), 0, 0)),
            pl.BlockSpec((kmix, d), lambda i: (0, 0)),
            pl.BlockSpec((MEM_W, d), lambda i: (0, 0)),
        ],
        out_specs=pl.BlockSpec((step, d), lambda i: (i, 0)),
        out_shape=jax.ShapeDtypeStruct((rows, d), F32),
        compiler_params=pltpu.CompilerParams(
            dimension_semantics=("parallel",), vmem_limit_bytes=VMEM_LIMIT),
        name="mem_out",
    )(x, mix, proj, mem_k, mem_v, wmix_bf16, wmem_bf16)


def _dil_prompt_kernel(slope_ref, q_ref, kc_ref, kp_ref, vc_ref, vp_ref, o_ref, og_s, lse_s):
    win = pl.program_id(1)
    rows = q_ref.shape[0]
    blk = DIL_BLOCK
    slope = slope_ref[...]
    ri = lax.broadcasted_iota(jnp.int32, (blk, blk), 0)
    ci = lax.broadcasted_iota(jnp.int32, (blk, blk), 1)
    diff = (ri - ci).astype(F32)
    own_ok = (ri >= ci)[None]
    prev_ok = (ci >= ri)[None]
    not_first = win > 0
    scale = B_HD ** -0.5

    for g, (w, d) in enumerate(DIL_PATTERNS):
        per_res = rows // (blk * d)
        assert w // d == blk and rows % (blk * d) == 0 and per_res & (per_res - 1) == 0

        def rows_of(r, j):
            return pl.ds(j * blk * d + r, blk, stride=d) if d > 1 else pl.ds(j * blk, blk)

        own = [rows_of(r, j) for r in range(d) for j in range(per_res)]
        prv = [(kc_ref, vc_ref, rows_of(r, j - 1)) if j > 0
               else (kp_ref, vp_ref, rows_of(r, per_res - 1))
               for r in range(d) for j in range(per_res)]
        q = jnp.stack([q_ref[o, :] for o in own]).astype(BF16)
        k_own = jnp.stack([kc_ref[o, :] for o in own]).astype(BF16)
        v_own = jnp.stack([vc_ref[o, :] for o in own]).astype(BF16)
        k_prev = jnp.stack([kr[o, :] for kr, _, o in prv]).astype(BF16)
        v_prev = jnp.stack([vr[o, :] for _, vr, o in prv]).astype(BF16)
        bias_own = (-(slope * float(d)) * diff)[None]
        bias_prev = (-(slope * float(d)) * (diff + float(blk)))[None]
        s_own = jnp.einsum('bqd,bkd->bqk', q, k_own, preferred_element_type=F32) * scale + bias_own
        s_own = jnp.where(own_ok, s_own, NEG)
        s_prev = jnp.einsum('bqd,bkd->bqk', q, k_prev, preferred_element_type=F32) * scale + bias_prev
        bi = lax.broadcasted_iota(jnp.int32, (len(own), blk, blk), 0)
        has_prev = jnp.logical_or((bi & (per_res - 1)) != 0, not_first)
        s_prev = jnp.where(jnp.logical_and(prev_ok, has_prev), s_prev, NEG)
        m = jnp.maximum(jnp.max(s_own, axis=-1, keepdims=True),
                        jnp.max(s_prev, axis=-1, keepdims=True))
        e_own = jnp.exp(s_own - m)
        e_prev = jnp.exp(s_prev - m)
        den = jnp.sum(e_own, axis=-1, keepdims=True) + jnp.sum(e_prev, axis=-1, keepdims=True)
        o = (jnp.einsum('bqk,bkd->bqd', (e_own / den).astype(BF16), v_own,
                        preferred_element_type=F32)
             + jnp.einsum('bqk,bkd->bqd', (e_prev / den).astype(BF16), v_prev,
                          preferred_element_type=F32))
        lse = jnp.broadcast_to(m + jnp.log(den), o.shape)
        for b, rows_b in enumerate(own):
            og_s[g, rows_b, :] = o[b]
            lse_s[g, rows_b, :] = lse[b]

    top = jnp.maximum(jnp.maximum(lse_s[0], lse_s[1]), lse_s[2])
    wts = [jnp.exp(lse_s[g] - top) for g in range(len(DIL_PATTERNS))]
    tot = wts[0] + wts[1] + wts[2]
    o_ref[...] = ((wts[0] / tot) * og_s[0] + (wts[1] / tot) * og_s[1] + (wts[2] / tot) * og_s[2])


def _dil_prompt(proj_pad, kv_pad, nb, seq, slopes_b):
    win = DIL_WIN
    nw = seq // win
    assert seq % win == 0 and len(DIL_PATTERNS) == 3
    cur = lambda col0: (lambda n, w, h: (n * nw + w, col0 + h))
    prev = lambda col0: (lambda n, w, h: (n * nw + jnp.maximum(w - 1, 0), col0 + h))
    return pl.pallas_call(
        _dil_prompt_kernel,
        grid=(nb, nw, B_HEADS),
        in_specs=[
            pl.BlockSpec((None, 1, LANES), lambda n, w, h: (h, 0, 0)),
            pl.BlockSpec((win, LANES), cur(0)),
            pl.BlockSpec((win, LANES), cur(0)),
            pl.BlockSpec((win, LANES), prev(0)),
            pl.BlockSpec((win, LANES), cur(B_HEADS)),
            pl.BlockSpec((win, LANES), prev(B_HEADS)),
        ],
        out_specs=pl.BlockSpec((win, LANES), cur(0)),
        out_shape=jax.ShapeDtypeStruct((nb * seq, B_HEADS * LANES), F32),
        scratch_shapes=[
            pltpu.VMEM((len(DIL_PATTERNS), win, LANES), F32),
            pltpu.VMEM((len(DIL_PATTERNS), win, LANES), F32),
        ],
        compiler_params=pltpu.CompilerParams(
            dimension_semantics=("parallel", "parallel", "parallel"),
            vmem_limit_bytes=VMEM_LIMIT),
        name="dil_prompt",
    )(slopes_b, proj_pad, kv_pad, kv_pad, kv_pad, kv_pad)


def _dil_sample_kernel(q_ref, kc_ref, vc_ref, kn_ref, vn_ref, bc_ref, bn_ref, hm_ref, o_ref,
                       knp_s, vnp_s, pc_s, *, row_lo):
    t = q_ref.shape[0]
    nt = (((1,), (1,)), ((), ()))
    scale = B_HD ** -0.5
    hm = hm_ref[...]
    qbd = (jnp.concatenate([q_ref[...]] * B_HEADS, axis=0) * hm).astype(BF16)
    knp_s[...] = jnp.zeros_like(knp_s)
    vnp_s[...] = jnp.zeros_like(vnp_s)
    knp_s[0:t, :] = kn_ref[...]
    vnp_s[0:t, :] = vn_ref[...]
    s_c = lax.dot_general(qbd, kc_ref[...].astype(BF16), nt, preferred_element_type=F32) * scale
    s_n = lax.dot_general(qbd, knp_s[...].astype(BF16), nt, preferred_element_type=F32) * scale
    stats = []
    for g in range(len(DIL_PATTERNS)):
        lo = row_lo[g]
        bc = bc_ref[g, :, lo:]
        bn = bn_ref[g]
        sc = jnp.where(bc > 0.5 * NEG, s_c[:, lo:] + bc, NEG)
        sn = jnp.where(bn > 0.5 * NEG, s_n + bn, NEG)
        m = jnp.maximum(jnp.max(sc, axis=-1, keepdims=True), jnp.max(sn, axis=-1, keepdims=True))
        ec = jnp.exp(sc - m)
        en = jnp.exp(sn - m)
        den = jnp.sum(ec, axis=-1, keepdims=True) + jnp.sum(en, axis=-1, keepdims=True)
        stats.append((ec / den, en / den, m + jnp.log(den)))
    top = jnp.maximum(jnp.maximum(stats[0][2], stats[1][2]), stats[2][2])
    wts = [jnp.exp(st[2] - top) for st in stats]
    tot = wts[0] + wts[1] + wts[2]
    pc_s[...] = jnp.zeros_like(pc_s)
    pn = jnp.zeros((B_HEADS * t, knp_s.shape[0]), F32)
    for g in range(len(DIL_PATTERNS)):
        lo = row_lo[g]
        alpha = wts[g] / tot
        pc_s[:, lo:] += alpha * stats[g][0]
        pn = pn + alpha * stats[g][1]
    o = (jnp.dot(pc_s[...].astype(BF16), vc_ref[...].astype(BF16), preferred_element_type=F32)
         + jnp.dot(pn.astype(BF16), vnp_s[...].astype(BF16), preferred_element_type=F32))
    o_ref[...] = jnp.sum((o * hm).reshape(B_HEADS, t, MIX_W), axis=0)


def _dil_sample_keep(past, t):
    needed = np.zeros(past, bool)
    for w, d in DIL_PATTERNS:
        for i in range(t):
            idx = past + i - np.arange(0, w + 1, d)
            needed[idx[(idx >= 0) & (idx < past)]] = True
    period = max(d for _, d in DIL_PATTERNS)
    reach = sorted(w for w, _ in DIL_PATTERNS)[-2]
    split = max(0, (past - reach) // (period * SUBLANES) * (period * SUBLANES))
    head = needed[:split].reshape(-1, period)
    m = int(head[0].sum()) if split else 0
    regular = (split > 0 and (head == head[0]).all() and head[0, :m].all()
               and (split // period * m + past - split) % LANES == 0)
    return (split, m, period) if regular else (0, 0, 1)


def _dil_sample_compact(cache, keep):
    nb, past = cache.shape[:2]
    split, m, period = keep
    rows = np.arange(past)
    if split == 0:
        return cache.reshape(nb, past, MIX_W), rows
    head = cache[:, :split].reshape(nb, split // period, period, MIX_W)[:, :, :m]
    kept = jnp.concatenate([head.reshape(nb, split // period * m, MIX_W),
                            cache[:, split:].reshape(nb, past - split, MIX_W)], axis=1)
    rows = np.concatenate([rows[:split].reshape(-1, period)[:, :m].ravel(), rows[split:]])
    return kept, rows


def _dil_sample_tables(key_rows, past, t):
    slopes = np.asarray([2.0 ** (-8.0 * (h + 1) / B_HEADS) for h in range(B_HEADS)], np.float32)
    n_keep = len(key_rows)
    qpos = past + np.arange(t)
    kpos = np.concatenate([key_rows, past + np.arange(LANES)])
    exists = np.concatenate([np.ones(n_keep, bool), np.arange(LANES) < t])
    delta = qpos[:, None] - kpos[None, :]
    bias = np.full((len(DIL_PATTERNS), B_HEADS, t, n_keep + LANES), NEG, np.float32)
    col_lo = []
    for g, (w, d) in enumerate(DIL_PATTERNS):
        ok = (delta >= 0) & (delta <= w) & (delta % d == 0) & exists[None, :]
        vals = -slopes[:, None, None] * delta[None].astype(np.float32)
        bias[g] = np.where(ok[None], vals, np.float32(NEG))
        col_lo.append(int(np.searchsorted(key_rows, past - w)) // LANES * LANES)
    bias = bias.reshape(len(DIL_PATTERNS), B_HEADS * t, n_keep + LANES)
    hm = (np.arange(MIX_W)[None, :] // B_HD == np.arange(B_HEADS * t)[:, None] // t)
    return bias[:, :, :n_keep], bias[:, :, n_keep:], hm.astype(np.float32), tuple(col_lo)


def _dil_sample(proj, kv_new, cache_k, cache_v, key_rows, past_rows):
    nb, past, _ = cache_k.shape
    t = proj.shape[0] // nb
    assert t % SUBLANES == 0 and t <= LANES and past % LANES == 0 and len(DIL_PATTERNS) == 3
    assert past == len(key_rows)
    bias_c, bias_n, hm, col_lo = _dil_sample_tables(key_rows, past_rows, t)
    ng = len(DIL_PATTERNS)
    return pl.pallas_call(
        functools.partial(_dil_sample_kernel, row_lo=col_lo),
        grid=(nb,),
        in_specs=[
            pl.BlockSpec((t, MIX_W), lambda b: (b, 0)),
            pl.BlockSpec((None, past, MIX_W), lambda b: (b, 0, 0)),
            pl.BlockSpec((None, past, MIX_W), lambda b: (b, 0, 0)),
            pl.BlockSpec((t, MIX_W), lambda b: (b, 0)),
            pl.BlockSpec((t, MIX_W), lambda b: (b, 1)),
            pl.BlockSpec((ng, B_HEADS * t, past), lambda b: (0, 0, 0)),
            pl.BlockSpec((ng, B_HEADS * t, LANES), lambda b: (0, 0, 0)),
            pl.BlockSpec((B_HEADS * t, MIX_W), lambda b: (0, 0)),
        ],
        out_specs=pl.BlockSpec((t, MIX_W), lambda b: (b, 0)),
        out_shape=jax.ShapeDtypeStruct((nb * t, MIX_W), F32),
        scratch_shapes=[
            pltpu.VMEM((LANES, MIX_W), F32),
            pltpu.VMEM((LANES, MIX_W), F32),
            pltpu.VMEM((B_HEADS * t, past), F32),
        ],
        compiler_params=pltpu.CompilerParams(
            dimension_semantics=("parallel",), vmem_limit_bytes=VMEM_LIMIT),
        name="dil_sample",
    )(proj, cache_k, cache_v, kv_new, kv_new, jnp.asarray(bias_c), jnp.asarray(bias_n),
      jnp.asarray(hm))


def _pad_head_cols(w):
    d = w.shape[0]
    w = jnp.pad(w.reshape(d, B_HEADS, B_HD), ((0, 0), (0, 0), (0, LANES - B_HD)))
    return w.reshape(d, B_HEADS * LANES)


def _unpad_heads(a):
    return a.reshape(a.shape[:-1] + (B_HEADS, LANES))[..., :B_HD]


def kernel(x_prompt, x_sample, state_pool, cache_win_k, cache_win_v, cache_mem_k, cache_mem_v,
           mem_prompt, norm_mix, w_in, pool_w, pool_scale, norm_mem, w_mem_kv, w_out,
           norm_kv, w_kv, norm_ffn, peer_wq, peer_subkeys, peer_u, peer_v, norm_final):
    nb_p, seq, d = x_prompt.shape
    nb_s, dec_seq, _ = x_sample.shape
    n_mem = mem_prompt.shape[1]
    past = cache_win_k.shape[1]
    tp = nb_p * seq
    ts = nb_s * dec_seq

    mem_flat = mem_prompt.reshape(nb_p * n_mem, d)
    mkv = [_norm_matmul(mem_flat, norm_mem[l], w_mem_kv[l].astype(BF16)) for l in range(DEPTH)]
    mem_k_p = jnp.stack([m[:, :MEM_W].reshape(nb_p, n_mem, MEM_W) for m in mkv], axis=0)
    mem_v_p = jnp.stack([m[:, MEM_W:].reshape(nb_p, n_mem, MEM_W) for m in mkv], axis=0)
    mem_k_s = cache_mem_k.reshape(DEPTH, nb_s, n_mem, MEM_W)
    mem_v_s = cache_mem_v.reshape(DEPTH, nb_s, n_mem, MEM_W)

    slopes_b = jnp.broadcast_to(
        jnp.asarray([2.0 ** (-8.0 * (h + 1) / B_HEADS) for h in range(B_HEADS)], F32)[:, None, None],
        (B_HEADS, 1, LANES))
    keep = _dil_sample_keep(past, dec_seq)
    cache_k, key_rows = _dil_sample_compact(cache_win_k, keep)
    cache_v, _ = _dil_sample_compact(cache_win_v, keep)

    xp = x_prompt.reshape(tp, d)
    xs = x_sample.reshape(ts, d)
    pool_p, pool_s = [], []
    kv_p = kv_s = None
    for l in range(DEPTH):
        w_in_l = w_in[l].astype(BF16)
        w_out_l = w_out[l].astype(BF16)
        proj_s = _norm_matmul(xs, norm_mix[l], w_in_l)
        if l < N_A:
            proj_p = _norm_matmul(xp, norm_mix[l], w_in_l)
            wbd = jax.scipy.linalg.block_diag(*[pool_w[l, g] for g in range(POOL_GROUPS)]).astype(BF16)
            mix_p, last_p = _pool_prompt(proj_p, nb_p, seq, wbd, pool_scale[l])
            pool_p.append(last_p[:, POOL_HALO - POOL_STATE:])
            mix_s, new_state = _pool_sample(proj_s, nb_s, dec_seq, state_pool[l], wbd,
                                            pool_scale[l], PAST_LEN)
            pool_s.append(new_state)
            wmix_p = w_out_l[:MIX_W]
            qm_blk_p = MIX_W // MEM_W
        else:
            w_in_pad = jnp.concatenate([_pad_head_cols(w_in[l][:, :MIX_W]), w_in[l][:, MIX_W:]],
                                       axis=1).astype(BF16)
            proj_p = _norm_matmul(xp, norm_mix[l], w_in_pad)
            mix_p = _dil_prompt(proj_p, kv_p, nb_p, seq, slopes_b)
            mix_s = _dil_sample(proj_s, kv_s, cache_k, cache_v, key_rows, past)
            wmix_p = _pad_head_cols(w_out[l][:MIX_W].T).T.astype(BF16)
            qm_blk_p = B_HEADS * LANES // MEM_W
        xp = _mem_out(xp, mix_p, proj_p, qm_blk_p, mem_k_p[l], mem_v_p[l], wmix_p,
                      w_out_l[MIX_W:], 1, TOK_TILE)
        xs = _mem_out(xs, mix_s, proj_s, MIX_W // MEM_W, mem_k_s[l], mem_v_s[l], w_out_l[:MIX_W],
                      w_out_l[MIX_W:], SAMPLE_GROUP, dec_seq)
        peer_w = (norm_ffn[l], peer_wq[l].T.astype(BF16), peer_subkeys[l].astype(BF16),
                  peer_u[l].astype(BF16), peer_v[l].T.astype(BF16))
        xp = _peer(xp, *peer_w)
        xs = _peer(xs, *peer_w)
        if l == N_A - 1:
            w_kv_pad = jnp.concatenate([_pad_head_cols(w_kv[:, :MIX_W]),
                                        _pad_head_cols(w_kv[:, MIX_W:])], axis=1).astype(BF16)
            kv_p = _norm_matmul(xp, norm_kv, w_kv_pad)
            kv_s = _norm_matmul(xs, norm_kv, w_kv.astype(BF16))
    y_p = _final_norm(xp, norm_final)
    y_s = _final_norm(xs, norm_final)
    keep = min(WINDOW_MAX, seq)
    kv_p4 = kv_p.reshape(nb_p, seq, 2 * B_HEADS * LANES)[:, seq - keep:]
    return (y_p.reshape(nb_p, seq, d), y_s.reshape(nb_s, dec_seq, d),
            jnp.stack(pool_p, axis=0),
            _unpad_heads(kv_p4[..., :B_HEADS * LANES]), _unpad_heads(kv_p4[..., B_HEADS * LANES:]),
            mem_k_p.reshape(DEPTH, nb_p, n_mem, MEM_HEADS, MEM_HD),
            mem_v_p.reshape(DEPTH, nb_p, n_mem, MEM_HEADS, MEM_HD),
            jnp.stack(pool_s, axis=0),
            kv_s[:, :MIX_W].reshape(nb_s, dec_seq, B_HEADS, B_HD),
            kv_s[:, MIX_W:].reshape(nb_s, dec_seq, B_HEADS, B_HD))
```

```python
import functools
import math

import numpy as np
import jax
import jax.numpy as jnp
from jax import lax
from jax.experimental import pallas as pl
from jax.experimental.pallas import tpu as pltpu

F32 = jnp.float32
BF16 = jnp.bfloat16

D_MODEL = 1024
DEPTH = 4
N_A = DEPTH // 2
MIX_W = 3 * D_MODEL // 4
MEM_HEADS = 4
MEM_HD = (D_MODEL - MIX_W) // MEM_HEADS
MEM_W = MEM_HEADS * MEM_HD
POOL_WINDOWS = (2, 4, 8, 16)
POOL_GROUPS = len(POOL_WINDOWS)
POOL_GW = MIX_W // POOL_GROUPS
POOL_STATE = max(POOL_WINDOWS) - 1
B_HEADS = 8
B_HD = MIX_W // B_HEADS
DIL_PATTERNS = ((128, 1), (512, 4), (2048, 16))
WINDOW_MAX = max(w for w, _ in DIL_PATTERNS)
PEER_HEADS = 8
PEER_NKEYS = 128
PEER_EXPERTS = PEER_NKEYS * PEER_NKEYS
PEER_DK = 256
PEER_TOPK = 16
PAST_LEN = 2048
EPS = 1e-6
NEG = -1e30

LANES = 128
SUBLANES = 8
TOK_TILE = 512
PEER_TOK = 512
PEER_CHUNK_KEYS = 16
PEER_CHUNK = PEER_CHUNK_KEYS * PEER_NKEYS
PEER_PART_KEYS = (4, 4, 4, 4)
PEER_STAIR = tuple(PEER_TOPK // (k + 1) for k in range(PEER_TOPK))
PEER_NCAND = -(-sum(PEER_STAIR) // SUBLANES) * SUBLANES
PEER_GUARD = 2.0 ** -21
POOL_HALO = 16
DIL_BLOCK = 128
DIL_WIN = 2048
SAMPLE_GROUP = 16
VMEM_LIMIT = 56 * 1024 * 1024


def _rms(x, g):
    r = lax.rsqrt(jnp.mean(x * x, axis=-1, keepdims=True) + EPS)
    return (x * r) * g


def _norm_matmul_kernel(x_ref, g_ref, w_ref, o_ref):
    h = _rms(x_ref[...], g_ref[...])
    o_ref[...] = jnp.dot(h.astype(BF16), w_ref[...], preferred_element_type=F32)


def _norm_matmul(x, g, w_bf16, tile=TOK_TILE):
    t, d = x.shape
    n = w_bf16.shape[1]
    tile = min(tile, t)
    assert t % tile == 0
    return pl.pallas_call(
        _norm_matmul_kernel,
        grid=(t // tile,),
        in_specs=[
            pl.BlockSpec((tile, d), lambda i: (i, 0)),
            pl.BlockSpec((1, d), lambda i: (0, 0)),
            pl.BlockSpec((d, n), lambda i: (0, 0)),
        ],
        out_specs=pl.BlockSpec((tile, n), lambda i: (i, 0)),
        out_shape=jax.ShapeDtypeStruct((t, n), F32),
        compiler_params=pltpu.CompilerParams(
            dimension_semantics=("parallel",), vmem_limit_bytes=VMEM_LIMIT),
        name="norm_matmul",
    )(x, g.reshape(1, d), w_bf16)


def _final_norm_kernel(x_ref, g_ref, o_ref):
    o_ref[...] = _rms(x_ref[...], g_ref[...])


def _final_norm(x, g, tile=TOK_TILE):
    t, d = x.shape
    assert t % tile == 0
    return pl.pallas_call(
        _final_norm_kernel,
        grid=(t // tile,),
        in_specs=[
            pl.BlockSpec((tile, d), lambda i: (i, 0)),
            pl.BlockSpec((1, d), lambda i: (0, 0)),
        ],
        out_specs=pl.BlockSpec((tile, d), lambda i: (i, 0)),
        out_shape=jax.ShapeDtypeStruct((t, d), F32),
        compiler_params=pltpu.CompilerParams(dimension_semantics=("parallel",)),
        name="final_norm",
    )(x, g.reshape(1, d))


def _extract_top(s, out_ref, n):
    for k in range(n):
        m = jnp.max(s, axis=0, keepdims=True)
        out_ref[k:k + 1, :] = m
        s = jnp.where(s == m, -jnp.inf, s)


def _odd_even_merge_sort(n):
    pairs = []
    p = 1
    while p < n:
        k = p
        while k >= 1:
            for j in range(k % p, n - k, 2 * k):
                for i in range(min(k, n - j - k)):
                    if (i + j) // (2 * p) == (i + j + k) // (2 * p):
                        pairs.append((i + j, i + j + k))
            k //= 2
        p *= 2
    return pairs


def _sorted_top(s, out_ref):
    n = s.shape[0] // SUBLANES
    assert s.shape[0] == n * SUBLANES and n & (n - 1) == 0

    def exchange(v, i, j):
        v[i], v[j] = jnp.maximum(v[i], v[j]), jnp.minimum(v[i], v[j])

    v = [s[i * SUBLANES:(i + 1) * SUBLANES, :] for i in range(n)]
    for i, j in _odd_even_merge_sort(n):
        exchange(v, i, j)
    shift = SUBLANES // 2
    while shift >= 1:
        other = [pltpu.roll(t, shift, 0) for t in v]
        v = [jnp.maximum(v[i], other[n - 1 - i]) for i in range(n)]
        stride = n // 2
        while stride >= 1:
            for i in range(n):
                if i & stride == 0:
                    exchange(v, i, i + stride)
            stride //= 2
        shift //= 2
    for k in range(n):
        out_ref[k:k + 1, :] = v[k][0:1, :]


def _peer_kernel(x_ref, g_ref, wqt_ref, sk_ref, u_ref, vt_ref, o_ref,
                 ht_s, q_s, se_s, thr_s, e1r_s, a_s, b_s, cand_s, top_s,
                 sc_s, w_s, acc_s):
    c = pl.program_id(1)
    tok = x_ref.shape[0]
    n_lt = tok // LANES

    @pl.when(c == 0)
    def _route():
        h = _rms(x_ref[...], g_ref[...])
        ht_s[...] = h.T.astype(BF16)
        cand_s[...] = jnp.full(cand_s.shape, -jnp.inf, F32)

        def head_body(hd, carry):
            base = pl.multiple_of(hd * PEER_DK, PEER_DK)
            half = PEER_DK // 2
            row0 = pl.multiple_of(hd * PEER_NKEYS, PEER_NKEYS)
            q_s[...] = jnp.dot(wqt_ref[pl.ds(base, PEER_DK), :], ht_s[...],
                               preferred_element_type=F32).astype(BF16)
            s1 = jnp.dot(sk_ref[0], q_s[0:half, :], preferred_element_type=F32)
            s2 = jnp.dot(sk_ref[1], q_s[half:PEER_DK, :], preferred_element_type=F32)
            for lt in range(n_lt):
                ln = slice(lt * LANES, (lt + 1) * LANES)
                _sorted_top(s1[:, ln], a_s)
                _sorted_top(s2[:, ln], b_s)
                off = 0
                for k, width in enumerate(PEER_STAIR):
                    cand_s[off:off + width, :] = a_s[k:k + 1, :] + b_s[0:width, :]
                    off += width
                _extract_top(cand_s[...], top_s, PEER_TOPK)
                top = top_s[...]
                m0 = top[0:1, :]
                den = jnp.sum(jnp.exp(top - m0), axis=0, keepdims=True)
                tau = top[PEER_TOPK - 1:PEER_TOPK, :]
                s1_t = s1[:, ln]
                acc_s[pl.ds(row0, PEER_NKEYS), ln] = (
                    (tau - s1_t) - PEER_GUARD * (jnp.abs(tau) + jnp.abs(s1_t)))
                sc_s[pl.ds(PEER_HEADS * PEER_NKEYS + row0, PEER_NKEYS), ln] = (
                    (0.5 * jnp.exp(s1[:, ln] - a_s[0:1, :])) / den)
                se_s[lt, hd, 0, 0:PEER_NKEYS, :] = s2[:, ln]
                se_s[lt, hd, 1, 0:PEER_NKEYS, :] = jnp.exp(s2[:, ln] - b_s[0:1, :])
            return carry

        lax.fori_loop(0, PEER_HEADS, head_body, 0)

        for hd in range(PEER_HEADS):
            rows = pl.ds(hd, PEER_NKEYS, stride=PEER_HEADS)
            for lt in range(n_lt):
                ln = slice(lt * LANES, (lt + 1) * LANES)
                thr_s[lt, rows, :] = acc_s[hd * PEER_NKEYS:(hd + 1) * PEER_NKEYS, ln]
                e1r_s[lt, rows, :] = sc_s[(PEER_HEADS + hd) * PEER_NKEYS:
                                          (PEER_HEADS + hd + 1) * PEER_NKEYS, ln]
        acc_s[...] = jnp.zeros_like(acc_s)

    key0 = [sum(PEER_PART_KEYS[:p]) for p in range(len(PEER_PART_KEYS) + 1)]
    part_rows = [slice(key0[p] * PEER_NKEYS, key0[p + 1] * PEER_NKEYS)
                 for p in range(len(PEER_PART_KEYS))]

    def pre_dot(part):
        rows = part_rows[part]
        sc_s[rows, :] = jnp.dot(u_ref[rows, :], ht_s[...], preferred_element_type=F32)

    def build(part):
        for j in range(key0[part], key0[part + 1]):
            r0 = pl.multiple_of((c * PEER_CHUNK_KEYS + j) * PEER_HEADS, PEER_HEADS)
            rows = slice(j * PEER_NKEYS, (j + 1) * PEER_NKEYS)
            for lt in range(n_lt):
                ln = slice(lt * LANES, (lt + 1) * LANES)
                wgt = None
                thr_rows = thr_s[lt, pl.ds(r0, PEER_HEADS), :]
                e1_rows = e1r_s[lt, pl.ds(r0, PEER_HEADS), :]
                for hd in range(PEER_HEADS):
                    s2 = se_s[lt, hd, 0, 0:PEER_NKEYS, :]
                    p = e1_rows[hd:hd + 1, :] * se_s[lt, hd, 1, 0:PEER_NKEYS, :]
                    term = jnp.where(s2 >= thr_rows[hd:hd + 1, :], p, 0.0)
                    wgt = term if wgt is None else wgt + term
                pre = sc_s[rows, ln]
                act = pre * (1.0 + lax.erf(pre * (1.0 / math.sqrt(2.0))))
                w_s[rows, ln] = (wgt * act).astype(BF16)

    def out_dot(part):
        rows = part_rows[part]
        acc_s[...] += jnp.dot(vt_ref[:, rows], w_s[rows, :], preferred_element_type=F32)

    n_parts = len(PEER_PART_KEYS)
    pre_dot(0)
    for part in range(n_parts):
        if part + 1 < n_parts:
            pre_dot(part + 1)
        build(part)
        if part > 0:
            out_dot(part - 1)
    out_dot(n_parts - 1)

    @pl.when(c == pl.num_programs(1) - 1)
    def _finish():
        o_ref[...] = x_ref[...] + acc_s[...].T


def _peer(x, g, wqt_bf16, sk_bf16, u_bf16, vt_bf16, tok=PEER_TOK):
    t, d = x.shape
    assert t % tok == 0 and sum(PEER_PART_KEYS) == PEER_CHUNK_KEYS
    assert PEER_HEADS * PEER_NKEYS <= min(d, PEER_CHUNK // 2)
    n_chunks = PEER_EXPERTS // PEER_CHUNK
    hq = PEER_HEADS * PEER_DK
    return pl.pallas_call(
        _peer_kernel,
        grid=(t // tok, n_chunks),
        in_specs=[
            pl.BlockSpec((tok, d), lambda i, c: (i, 0), pipeline_mode=pl.Buffered(1)),
            pl.BlockSpec((1, d), lambda i, c: (0, 0)),
            pl.BlockSpec((hq, d), lambda i, c: (0, 0), pipeline_mode=pl.Buffered(1)),
            pl.BlockSpec((2, PEER_NKEYS, PEER_DK // 2), lambda i, c: (0, 0, 0)),
            pl.BlockSpec((PEER_CHUNK, d), lambda i, c: (c, 0)),
            pl.BlockSpec((d, PEER_CHUNK), lambda i, c: (0, c)),
        ],
        out_specs=pl.BlockSpec((tok, d), lambda i, c: (i, 0)),
        out_shape=jax.ShapeDtypeStruct((t, d), F32),
        scratch_shapes=[
            pltpu.VMEM((d, tok), BF16),
            pltpu.VMEM((PEER_DK, tok), BF16),
            pltpu.VMEM((tok // LANES, PEER_HEADS, 2, PEER_NKEYS + SUBLANES, LANES), F32),
            pltpu.VMEM((tok // LANES, PEER_NKEYS * PEER_HEADS, LANES), F32),
            pltpu.VMEM((tok // LANES, PEER_NKEYS * PEER_HEADS, LANES), F32),
            pltpu.VMEM((PEER_TOPK, LANES), F32),
            pltpu.VMEM((PEER_TOPK, LANES), F32),
            pltpu.VMEM((PEER_NCAND, LANES), F32),
            pltpu.VMEM((PEER_TOPK, LANES), F32),
            pltpu.VMEM((PEER_CHUNK, tok), F32),
            pltpu.VMEM((PEER_CHUNK, tok), BF16),
            pltpu.VMEM((d, tok), F32),
        ],
        compiler_params=pltpu.CompilerParams(
            dimension_semantics=("parallel", "arbitrary"), vmem_limit_bytes=VMEM_LIMIT),
        name="peer",
    )(x, g.reshape(1, d), wqt_bf16, sk_bf16, u_bf16, vt_bf16)


def _pool_tile_plan(ct):
    lo = ct * LANES
    hi = lo + LANES - 1
    return [(POOL_WINDOWS[g], (g + 1) * POOL_GW) for g in range(lo // POOL_GW, hi // POOL_GW + 1)]


def _pool_diff_tile(load_shifted, z_tile, pos1, ct):
    plan = _pool_tile_plan(ct)
    wanted = {w for w, _ in plan}
    acc = z_tile
    snaps = {}
    for j in range(1, max(wanted)):
        acc = acc + load_shifted(j)
        if j + 1 in wanted:
            snaps[j + 1] = acc
    if len(plan) == 1:
        w = plan[0][0]
        win = snaps[w]
        cnt = jnp.minimum(float(w), pos1)
    else:
        (w_lo, edge), (w_hi, _) = plan
        lane = lax.broadcasted_iota(jnp.int32, (1,) * (z_tile.ndim - 1) + (LANES,), z_tile.ndim - 1)
        in_lo = lane + ct * LANES < edge
        win = jnp.where(in_lo, snaps[w_lo], snaps[w_hi])
        cnt = jnp.minimum(jnp.where(in_lo, float(w_lo), float(w_hi)), pos1)
    return win / cnt - z_tile


def _pool_prompt_kernel(z_ref, wbd_ref, scale_ref, mix_ref, state_ref, zc_s, d_s):
    i = pl.program_id(1)
    tile = z_ref.shape[0]

    @pl.when(i == 0)
    def _start():
        zc_s[0:POOL_HALO, :] = jnp.zeros((POOL_HALO, MIX_W), F32)

    zc_s[POOL_HALO:POOL_HALO + tile, :] = z_ref[...]
    pos1 = (i * tile + lax.broadcasted_iota(jnp.int32, (tile, 1), 0) + 1).astype(F32)
    for ct in range(MIX_W // LANES):
        ln = slice(ct * LANES, (ct + 1) * LANES)
        diff = _pool_diff_tile(
            lambda j: zc_s[POOL_HALO - j:POOL_HALO - j + tile, ln], z_ref[:, ln], pos1, ct)
        d_s[:, ln] = diff.astype(BF16)
    mix_ref[...] = jnp.dot(d_s[...], wbd_ref[...], preferred_element_type=F32) * scale_ref[...]
    last = z_ref[tile - POOL_HALO:tile, :]
    state_ref[...] = last
    zc_s[0:POOL_HALO, :] = last


def _pool_prompt(proj, nb, seq, wbd_bf16, scale, tile=TOK_TILE):
    nt = seq // tile
    assert seq % tile == 0 and tile >= POOL_HALO
    return pl.pallas_call(
        _pool_prompt_kernel,
        grid=(nb, nt),
        in_specs=[
            pl.BlockSpec((tile, MIX_W), lambda n, i: (n * nt + i, 0)),
            pl.BlockSpec((MIX_W, MIX_W), lambda n, i: (0, 0)),
            pl.BlockSpec((1, MIX_W), lambda n, i: (0, 0)),
        ],
        out_specs=[
            pl.BlockSpec((tile, MIX_W), lambda n, i: (n * nt + i, 0)),
            pl.BlockSpec((None, POOL_HALO, MIX_W), lambda n, i: (n, 0, 0)),
        ],
        out_shape=[
            jax.ShapeDtypeStruct((nb * seq, MIX_W), F32),
            jax.ShapeDtypeStruct((nb, POOL_HALO, MIX_W), F32),
        ],
        scratch_shapes=[
            pltpu.VMEM((POOL_HALO + tile, MIX_W), F32),
            pltpu.VMEM((tile, MIX_W), BF16),
        ],
        compiler_params=pltpu.CompilerParams(
            dimension_semantics=("arbitrary", "arbitrary"), vmem_limit_bytes=VMEM_LIMIT),
        name="pool_prompt",
    )(proj, wbd_bf16, scale.reshape(1, MIX_W))


def _pool_sample_kernel(z_ref, st_ref, wbd_ref, scale_ref, mix_ref, nst_ref, zc_s, d_s, *, pos0):
    grp, n_state, _ = st_ref.shape
    t = z_ref.shape[0] // grp
    zc_s[:, POOL_HALO - n_state:POOL_HALO, :] = st_ref[...]
    zc_s[:, POOL_HALO:POOL_HALO + t, :] = z_ref[...].reshape(grp, t, MIX_W)
    pos1 = (pos0 + lax.broadcasted_iota(jnp.int32, (1, t, 1), 1) + 1).astype(F32)
    for ct in range(MIX_W // LANES):
        ln = slice(ct * LANES, (ct + 1) * LANES)
        diff = _pool_diff_tile(
            lambda j: zc_s[:, POOL_HALO - j:POOL_HALO - j + t, ln],
            zc_s[:, POOL_HALO:POOL_HALO + t, ln], pos1, ct)
        d_s[:, ln] = diff.reshape(grp * t, LANES).astype(BF16)
    mix_ref[...] = jnp.dot(d_s[...], wbd_ref[...], preferred_element_type=F32) * scale_ref[...]
    nst_ref[...] = zc_s[:, POOL_HALO + t - n_state:POOL_HALO + t, :]


def _pool_sample(proj, nb, t, state, wbd_bf16, scale, pos0, grp=SAMPLE_GROUP):
    n_state = state.shape[1]
    assert nb % grp == 0 and t % SUBLANES == 0
    assert max(POOL_WINDOWS) - 1 <= n_state <= POOL_HALO - 1
    return pl.pallas_call(
        functools.partial(_pool_sample_kernel, pos0=pos0),
        grid=(nb // grp,),
        in_specs=[
            pl.BlockSpec((grp * t, MIX_W), lambda i: (i, 0)),
            pl.BlockSpec((grp, n_state, MIX_W), lambda i: (i, 0, 0)),
            pl.BlockSpec((MIX_W, MIX_W), lambda i: (0, 0)),
            pl.BlockSpec((1, MIX_W), lambda i: (0, 0)),
        ],
        out_specs=[
            pl.BlockSpec((grp * t, MIX_W), lambda i: (i, 0)),
            pl.BlockSpec((grp, n_state, MIX_W), lambda i: (i, 0, 0)),
        ],
        out_shape=[
            jax.ShapeDtypeStruct((nb * t, MIX_W), F32),
            jax.ShapeDtypeStruct((nb, n_state, MIX_W), F32),
        ],
        scratch_shapes=[
            pltpu.VMEM((grp, POOL_HALO + t, MIX_W), F32),
            pltpu.VMEM((grp * t, MIX_W), BF16),
        ],
        compiler_params=pltpu.CompilerParams(
            dimension_semantics=("parallel",), vmem_limit_bytes=VMEM_LIMIT),
        name="pool_sample",
    )(proj, state, wbd_bf16, scale.reshape(1, MIX_W))


def _mem_out_kernel(x_ref, mix_ref, qm_ref, mk_ref, mv_ref, wmix_ref, wmem_ref, o_ref):
    grp = mk_ref.shape[0]
    rows = x_ref.shape[0]
    tq = rows // grp
    acc = x_ref[...] + jnp.dot(mix_ref[...].astype(BF16), wmix_ref[...],
                               preferred_element_type=F32)
    for h in range(MEM_HEADS):
        cols = slice(h * MEM_HD, (h + 1) * MEM_HD)
        q = qm_ref[:, cols].astype(BF16).reshape(grp, tq, MEM_HD)
        k = mk_ref[:, :, cols].astype(BF16)
        v = mv_ref[:, :, cols].astype(BF16)
        s = jnp.einsum('bqd,bkd->bqk', q, k, preferred_element_type=F32) * (MEM_HD ** -0.5)
        e = jnp.exp(s - jnp.max(s, axis=-1, keepdims=True))
        p = e / jnp.sum(e, axis=-1, keepdims=True)
        o = jnp.einsum('bqk,bkd->bqd', p.astype(BF16), v, preferred_element_type=F32)
        acc = acc + jnp.dot(o.reshape(rows, MEM_HD).astype(BF16), wmem_ref[cols, :],
                            preferred_element_type=F32)
    o_ref[...] = acc


def _mem_out(x, mix, proj, qm_col_blk, mem_k, mem_v, wmix_bf16, wmem_bf16, grp, tq):
    rows, d = x.shape
    kmix = mix.shape[1]
    n_seq, n_mem, _ = mem_k.shape
    step = grp * tq
    per_seq = rows // n_seq
    assert rows % step == 0 and tq % SUBLANES == 0
    assert (grp == 1 and per_seq % tq == 0) or per_seq == tq
    tiles_per_seq = per_seq // tq
    seq_blk = (lambda i: i // tiles_per_seq) if grp == 1 else (lambda i: i)
    return pl.pallas_call(
        _mem_out_kernel,
        grid=(rows // step,),
        in_specs=[
            pl.BlockSpec((step, d), lambda i: (i, 0)),
            pl.BlockSpec((step, kmix), lambda i: (i, 0)),
            pl.BlockSpec((step, MEM_W), lambda i: (i, qm_col_blk)),
            pl.BlockSpec((grp, n_mem, MEM_W), lambda i: (seq_blk(i), 0, 0)),
            pl.BlockSpec((grp, n_mem, MEM_W), lambda i: (seq_blk(i), 0, 0)),
            pl.BlockSpec((kmix, d), lambda i: (0, 0)),
            pl.BlockSpec((MEM_W, d), lambda i: (0, 0)),
        ],
        out_specs=pl.BlockSpec((step, d), lambda i: (i, 0)),
        out_shape=jax.ShapeDtypeStruct((rows, d), F32),
        compiler_params=pltpu.CompilerParams(
            dimension_semantics=("parallel",), vmem_limit_bytes=VMEM_LIMIT),
        name="mem_out",
    )(x, mix, proj, mem_k, mem_v, wmix_bf16, wmem_bf16)


def _dil_prompt_kernel(slope_ref, q_ref, kc_ref, kp_ref, vc_ref, vp_ref, o_ref, og_s, lse_s):
    win = pl.program_id(1)
    rows = q_ref.shape[0]
    blk = DIL_BLOCK
    slope = slope_ref[...]
    ri = lax.broadcasted_iota(jnp.int32, (blk, blk), 0)
    ci = lax.broadcasted_iota(jnp.int32, (blk, blk), 1)
    diff = (ri - ci).astype(F32)
    own_ok = (ri >= ci)[None]
    prev_ok = (ci >= ri)[None]
    not_first = win > 0
    scale = B_HD ** -0.5

    for g, (w, d) in enumerate(DIL_PATTERNS):
        per_res = rows // (blk * d)
        assert w // d == blk and rows % (blk * d) == 0 and per_res & (per_res - 1) == 0

        def rows_of(r, j):
            return pl.ds(j * blk * d + r, blk, stride=d) if d > 1 else pl.ds(j * blk, blk)

        own = [rows_of(r, j) for r in range(d) for j in range(per_res)]
        prv = [(kc_ref, vc_ref, rows_of(r, j - 1)) if j > 0
               else (kp_ref, vp_ref, rows_of(r, per_res - 1))
               for r in range(d) for j in range(per_res)]
        q = jnp.stack([q_ref[o, :] for o in own]).astype(BF16)
        k_own = jnp.stack([kc_ref[o, :] for o in own]).astype(BF16)
        v_own = jnp.stack([vc_ref[o, :] for o in own]).astype(BF16)
        k_prev = jnp.stack([kr[o, :] for kr, _, o in prv]).astype(BF16)
        v_prev = jnp.stack([vr[o, :] for _, vr, o in prv]).astype(BF16)
        bias_own = (-(slope * float(d)) * diff)[None]
        bias_prev = (-(slope * float(d)) * (diff + float(blk)))[None]
        s_own = jnp.einsum('bqd,bkd->bqk', q, k_own, preferred_element_type=F32) * scale + bias_own
        s_own = jnp.where(own_ok, s_own, NEG)
        s_prev = jnp.einsum('bqd,bkd->bqk', q, k_prev, preferred_element_type=F32) * scale + bias_prev
        bi = lax.broadcasted_iota(jnp.int32, (len(own), blk, blk), 0)
        has_prev = jnp.logical_or((bi & (per_res - 1)) != 0, not_first)
        s_prev = jnp.where(jnp.logical_and(prev_ok, has_prev), s_prev, NEG)
        m = jnp.maximum(jnp.max(s_own, axis=-1, keepdims=True),
                        jnp.max(s_prev, axis=-1, keepdims=True))
        e_own = jnp.exp(s_own - m)
        e_prev = jnp.exp(s_prev - m)
        den = jnp.sum(e_own, axis=-1, keepdims=True) + jnp.sum(e_prev, axis=-1, keepdims=True)
        o = (jnp.einsum('bqk,bkd->bqd', (e_own / den).astype(BF16), v_own,
                        preferred_element_type=F32)
             + jnp.einsum('bqk,bkd->bqd', (e_prev / den).astype(BF16), v_prev,
                          preferred_element_type=F32))
        lse = jnp.broadcast_to(m + jnp.log(den), o.shape)
        for b, rows_b in enumerate(own):
            og_s[g, rows_b, :] = o[b]
            lse_s[g, rows_b, :] = lse[b]

    top = jnp.maximum(jnp.maximum(lse_s[0], lse_s[1]), lse_s[2])
    wts = [jnp.exp(lse_s[g] - top) for g in range(len(DIL_PATTERNS))]
    tot = wts[0] + wts[1] + wts[2]
    o_ref[...] = ((wts[0] / tot) * og_s[0] + (wts[1] / tot) * og_s[1] + (wts[2] / tot) * og_s[2])


def _dil_prompt(proj_pad, kv_pad, nb, seq, slopes_b):
    win = DIL_WIN
    nw = seq // win
    assert seq % win == 0 and len(DIL_PATTERNS) == 3
    cur = lambda col0: (lambda n, w, h: (n * nw + w, col0 + h))
    prev = lambda col0: (lambda n, w, h: (n * nw + jnp.maximum(w - 1, 0), col0 + h))
    return pl.pallas_call(
        _dil_prompt_kernel,
        grid=(nb, nw, B_HEADS),
        in_specs=[
            pl.BlockSpec((None, 1, LANES), lambda n, w, h: (h, 0, 0)),
            pl.BlockSpec((win, LANES), cur(0)),
            pl.BlockSpec((win, LANES), cur(0)),
            pl.BlockSpec((win, LANES), prev(0)),
            pl.BlockSpec((win, LANES), cur(B_HEADS)),
            pl.BlockSpec((win, LANES), prev(B_HEADS)),
        ],
        out_specs=pl.BlockSpec((win, LANES), cur(0)),
        out_shape=jax.ShapeDtypeStruct((nb * seq, B_HEADS * LANES), F32),
        scratch_shapes=[
            pltpu.VMEM((len(DIL_PATTERNS), win, LANES), F32),
            pltpu.VMEM((len(DIL_PATTERNS), win, LANES), F32),
        ],
        compiler_params=pltpu.CompilerParams(
            dimension_semantics=("parallel", "parallel", "parallel"),
            vmem_limit_bytes=VMEM_LIMIT),
        name="dil_prompt",
    )(slopes_b, proj_pad, kv_pad, kv_pad, kv_pad, kv_pad)


def _dil_sample_kernel(q_ref, kc_ref, vc_ref, kn_ref, vn_ref, bc_ref, bn_ref, hm_ref, o_ref,
                       knp_s, vnp_s, pc_s, *, row_lo):
    t = q_ref.shape[0]
    nt = (((1,), (1,)), ((), ()))
    scale = B_HD ** -0.5
    hm = hm_ref[...]
    qbd = (jnp.concatenate([q_ref[...]] * B_HEADS, axis=0) * hm).astype(BF16)
    knp_s[...] = jnp.zeros_like(knp_s)
    vnp_s[...] = jnp.zeros_like(vnp_s)
    knp_s[0:t, :] = kn_ref[...]
    vnp_s[0:t, :] = vn_ref[...]
    s_c = lax.dot_general(qbd, kc_ref[...].astype(BF16), nt, preferred_element_type=F32) * scale
    s_n = lax.dot_general(qbd, knp_s[...].astype(BF16), nt, preferred_element_type=F32) * scale
    stats = []
    for g in range(len(DIL_PATTERNS)):
        lo = row_lo[g]
        bc = bc_ref[g, :, lo:]
        bn = bn_ref[g]
        sc = jnp.where(bc > 0.5 * NEG, s_c[:, lo:] + bc, NEG)
        sn = jnp.where(bn > 0.5 * NEG, s_n + bn, NEG)
        m = jnp.maximum(jnp.max(sc, axis=-1, keepdims=True), jnp.max(sn, axis=-1, keepdims=True))
        ec = jnp.exp(sc - m)
        en = jnp.exp(sn - m)
        den = jnp.sum(ec, axis=-1, keepdims=True) + jnp.sum(en, axis=-1, keepdims=True)
        stats.append((ec / den, en / den, m + jnp.log(den)))
    top = jnp.maximum(jnp.maximum(stats[0][2], stats[1][2]), stats[2][2])
    wts = [jnp.exp(st[2] - top) for st in stats]
    tot = wts[0] + wts[1] + wts[2]
    pc_s[...] = jnp.zeros_like(pc_s)
    pn = jnp.zeros((B_HEADS * t, knp_s.shape[0]), F32)
    for g in range(len(DIL_PATTERNS)):
        lo = row_lo[g]
        alpha = wts[g] / tot
        pc_s[:, lo:] += alpha * stats[g][0]
        pn = pn + alpha * stats[g][1]
    o = (jnp.dot(pc_s[...].astype(BF16), vc_ref[...].astype(BF16), preferred_element_type=F32)
         + jnp.dot(pn.astype(BF16), vnp_s[...].astype(BF16), preferred_element_type=F32))
    o_ref[...] = jnp.sum((o * hm).reshape(B_HEADS, t, MIX_W), axis=0)


def _dil_sample_tables(past, t):
    slopes = np.asarray([2.0 ** (-8.0 * (h + 1) / B_HEADS) for h in range(B_HEADS)], np.float32)
    qpos = past + np.arange(t)
    col = np.arange(past + LANES)
    exists = col < past + t
    delta = qpos[:, None] - col[None, :]
    bias = np.full((len(DIL_PATTERNS), B_HEADS, t, past + LANES), NEG, np.float32)
    col_lo = []
    for g, (w, d) in enumerate(DIL_PATTERNS):
        ok = (delta >= 0) & (delta <= w) & (delta % d == 0) & exists[None, :]
        vals = -slopes[:, None, None] * delta[None].astype(np.float32)
        bias[g] = np.where(ok[None], vals, np.float32(NEG))
        col_lo.append(max(0, (past - w) // LANES * LANES))
    bias = bias.reshape(len(DIL_PATTERNS), B_HEADS * t, past + LANES)
    hm = (np.arange(MIX_W)[None, :] // B_HD == np.arange(B_HEADS * t)[:, None] // t)
    return bias[:, :, :past], bias[:, :, past:], hm.astype(np.float32), tuple(col_lo)


def _dil_sample(proj, kv_new, cache_k, cache_v):
    nb, past, _ = cache_k.shape
    t = proj.shape[0] // nb
    assert t % SUBLANES == 0 and t <= LANES and past % LANES == 0 and len(DIL_PATTERNS) == 3
    bias_c, bias_n, hm, col_lo = _dil_sample_tables(past, t)
    ng = len(DIL_PATTERNS)
    return pl.pallas_call(
        functools.partial(_dil_sample_kernel, row_lo=col_lo),
        grid=(nb,),
        in_specs=[
            pl.BlockSpec((t, MIX_W), lambda b: (b, 0)),
            pl.BlockSpec((None, past, MIX_W), lambda b: (b, 0, 0)),
            pl.BlockSpec((None, past, MIX_W), lambda b: (b, 0, 0)),
            pl.BlockSpec((t, MIX_W), lambda b: (b, 0)),
            pl.BlockSpec((t, MIX_W), lambda b: (b, 1)),
            pl.BlockSpec((ng, B_HEADS * t, past), lambda b: (0, 0, 0)),
            pl.BlockSpec((ng, B_HEADS * t, LANES), lambda b: (0, 0, 0)),
            pl.BlockSpec((B_HEADS * t, MIX_W), lambda b: (0, 0)),
        ],
        out_specs=pl.BlockSpec((t, MIX_W), lambda b: (b, 0)),
        out_shape=jax.ShapeDtypeStruct((nb * t, MIX_W), F32),
        scratch_shapes=[
            pltpu.VMEM((LANES, MIX_W), F32),
            pltpu.VMEM((LANES, MIX_W), F32),
            pltpu.VMEM((B_HEADS * t, past), F32),
        ],
        compiler_params=pltpu.CompilerParams(
            dimension_semantics=("parallel",), vmem_limit_bytes=VMEM_LIMIT),
        name="dil_sample",
    )(proj, cache_k, cache_v, kv_new, kv_new, jnp.asarray(bias_c), jnp.asarray(bias_n),
      jnp.asarray(hm))


def _pad_head_cols(w):
    d = w.shape[0]
    w = jnp.pad(w.reshape(d, B_HEADS, B_HD), ((0, 0), (0, 0), (0, LANES - B_HD)))
    return w.reshape(d, B_HEADS * LANES)


def _unpad_heads(a):
    return a.reshape(a.shape[:-1] + (B_HEADS, LANES))[..., :B_HD]


def kernel(x_prompt, x_sample, state_pool, cache_win_k, cache_win_v, cache_mem_k, cache_mem_v,
           mem_prompt, norm_mix, w_in, pool_w, pool_scale, norm_mem, w_mem_kv, w_out,
           norm_kv, w_kv, norm_ffn, peer_wq, peer_subkeys, peer_u, peer_v, norm_final):
    nb_p, seq, d = x_prompt.shape
    nb_s, dec_seq, _ = x_sample.shape
    n_mem = mem_prompt.shape[1]
    past = cache_win_k.shape[1]
    tp = nb_p * seq
    ts = nb_s * dec_seq

    mem_flat = mem_prompt.reshape(nb_p * n_mem, d)
    mkv = [_norm_matmul(mem_flat, norm_mem[l], w_mem_kv[l].astype(BF16)) for l in range(DEPTH)]
    mem_k_p = jnp.stack([m[:, :MEM_W].reshape(nb_p, n_mem, MEM_W) for m in mkv], axis=0)
    mem_v_p = jnp.stack([m[:, MEM_W:].reshape(nb_p, n_mem, MEM_W) for m in mkv], axis=0)
    mem_k_s = cache_mem_k.reshape(DEPTH, nb_s, n_mem, MEM_W)
    mem_v_s = cache_mem_v.reshape(DEPTH, nb_s, n_mem, MEM_W)

    slopes_b = jnp.broadcast_to(
        jnp.asarray([2.0 ** (-8.0 * (h + 1) / B_HEADS) for h in range(B_HEADS)], F32)[:, None, None],
        (B_HEADS, 1, LANES))
    cache_k = cache_win_k.reshape(nb_s, past, MIX_W)
    cache_v = cache_win_v.reshape(nb_s, past, MIX_W)

    xp = x_prompt.reshape(tp, d)
    xs = x_sample.reshape(ts, d)
    pool_p, pool_s = [], []
    kv_p = kv_s = None
    for l in range(DEPTH):
        w_in_l = w_in[l].astype(BF16)
        w_out_l = w_out[l].astype(BF16)
        proj_s = _norm_matmul(xs, norm_mix[l], w_in_l)
        if l < N_A:
            proj_p = _norm_matmul(xp, norm_mix[l], w_in_l)
            wbd = jax.scipy.linalg.block_diag(*[pool_w[l, g] for g in range(POOL_GROUPS)]).astype(BF16)
            mix_p, last_p = _pool_prompt(proj_p, nb_p, seq, wbd, pool_scale[l])
            pool_p.append(last_p[:, POOL_HALO - POOL_STATE:])
            mix_s, new_state = _pool_sample(proj_s, nb_s, dec_seq, state_pool[l], wbd,
                                            pool_scale[l], PAST_LEN)
            pool_s.append(new_state)
            wmix_p = w_out_l[:MIX_W]
            qm_blk_p = MIX_W // MEM_W
        else:
            w_in_pad = jnp.concatenate([_pad_head_cols(w_in[l][:, :MIX_W]), w_in[l][:, MIX_W:]],
                                       axis=1).astype(BF16)
            proj_p = _norm_matmul(xp, norm_mix[l], w_in_pad)
            mix_p = _dil_prompt(proj_p, kv_p, nb_p, seq, slopes_b)
            mix_s = _dil_sample(proj_s, kv_s, cache_k, cache_v)
            wmix_p = _pad_head_cols(w_out[l][:MIX_W].T).T.astype(BF16)
            qm_blk_p = B_HEADS * LANES // MEM_W
        xp = _mem_out(xp, mix_p, proj_p, qm_blk_p, mem_k_p[l], mem_v_p[l], wmix_p,
                      w_out_l[MIX_W:], 1, TOK_TILE)
        xs = _mem_out(xs, mix_s, proj_s, MIX_W // MEM_W, mem_k_s[l], mem_v_s[l], w_out_l[:MIX_W],
                      w_out_l[MIX_W:], SAMPLE_GROUP, dec_seq)
        peer_w = (norm_ffn[l], peer_wq[l].T.astype(BF16), peer_subkeys[l].astype(BF16),
                  peer_u[l].astype(BF16), peer_v[l].T.astype(BF16))
        xp = _peer(xp, *peer_w)
        xs = _peer(xs, *peer_w)
        if l == N_A - 1:
            w_kv_pad = jnp.concatenate([_pad_head_cols(w_kv[:, :MIX_W]),
                                        _pad_head_cols(w_kv[:, MIX_W:])], axis=1).astype(BF16)
            kv_p = _norm_matmul(xp, norm_kv, w_kv_pad)
            kv_s = _norm_matmul(xs, norm_kv, w_kv.astype(BF16))
    y_p = _final_norm(xp, norm_final)
    y_s = _final_norm(xs, norm_final)
    keep = min(WINDOW_MAX, seq)
    kv_p4 = kv_p.reshape(nb_p, seq, 2 * B_HEADS * LANES)[:, seq - keep:]
    return (y_p.reshape(nb_p, seq, d), y_s.reshape(nb_s, dec_seq, d),
            jnp.stack(pool_p, axis=0),
            _unpad_heads(kv_p4[..., :B_HEADS * LANES]), _unpad_heads(kv_p4[..., B_HEADS * LANES:]),
            mem_k_p.reshape(DEPTH, nb_p, n_mem, MEM_HEADS, MEM_HD),
            mem_v_p.reshape(DEPTH, nb_p, n_mem, MEM_HEADS, MEM_HD),
            jnp.stack(pool_s, axis=0),
            kv_s[:, :MIX_W].reshape(nb_s, dec_seq, B_HEADS, B_HD),
            kv_s[:, MIX_W:].reshape(nb_s, dec_seq, B_HEADS, B_HD))
```

```python
import functools
import math

import numpy as np
import jax
import jax.numpy as jnp
from jax import lax
from jax.experimental import pallas as pl
from jax.experimental.pallas import tpu as pltpu

F32 = jnp.float32
BF16 = jnp.bfloat16

D_MODEL = 1024
DEPTH = 4
N_A = DEPTH // 2
MIX_W = 3 * D_MODEL // 4
MEM_HEADS = 4
MEM_HD = (D_MODEL - MIX_W) // MEM_HEADS
MEM_W = MEM_HEADS * MEM_HD
POOL_WINDOWS = (2, 4, 8, 16)
POOL_GROUPS = len(POOL_WINDOWS)
POOL_GW = MIX_W // POOL_GROUPS
POOL_STATE = max(POOL_WINDOWS) - 1
B_HEADS = 8
B_HD = MIX_W // B_HEADS
DIL_PATTERNS = ((128, 1), (512, 4), (2048, 16))
WINDOW_MAX = max(w for w, _ in DIL_PATTERNS)
PEER_HEADS = 8
PEER_NKEYS = 128
PEER_EXPERTS = PEER_NKEYS * PEER_NKEYS
PEER_DK = 256
PEER_TOPK = 16
PAST_LEN = 2048
EPS = 1e-6
NEG = -1e30

LANES = 128
SUBLANES = 8
TOK_TILE = 512
PEER_TOK = 512
PEER_CHUNK_KEYS = 16
PEER_CHUNK = PEER_CHUNK_KEYS * PEER_NKEYS
PEER_PART_KEYS = (8, 8)
PEER_STAIR = tuple(PEER_TOPK // (k + 1) for k in range(PEER_TOPK))
PEER_NCAND = -(-sum(PEER_STAIR) // SUBLANES) * SUBLANES
PEER_GUARD = 2.0 ** -21
POOL_HALO = 16
DIL_BLOCK = 128
DIL_WIN = 2048
SAMPLE_GROUP = 16
VMEM_LIMIT = 56 * 1024 * 1024


def _rms(x, g):
    r = lax.rsqrt(jnp.mean(x * x, axis=-1, keepdims=True) + EPS)
    return (x * r) * g


def _norm_matmul_kernel(x_ref, g_ref, w_ref, o_ref):
    h = _rms(x_ref[...], g_ref[...])
    o_ref[...] = jnp.dot(h.astype(BF16), w_ref[...], preferred_element_type=F32)


def _norm_matmul(x, g, w_bf16, tile=TOK_TILE):
    t, d = x.shape
    n = w_bf16.shape[1]
    tile = min(tile, t)
    assert t % tile == 0
    return pl.pallas_call(
        _norm_matmul_kernel,
        grid=(t // tile,),
        in_specs=[
            pl.BlockSpec((tile, d), lambda i: (i, 0)),
            pl.BlockSpec((1, d), lambda i: (0, 0)),
            pl.BlockSpec((d, n), lambda i: (0, 0)),
        ],
        out_specs=pl.BlockSpec((tile, n), lambda i: (i, 0)),
        out_shape=jax.ShapeDtypeStruct((t, n), F32),
        compiler_params=pltpu.CompilerParams(
            dimension_semantics=("parallel",), vmem_limit_bytes=VMEM_LIMIT),
        name="norm_matmul",
    )(x, g.reshape(1, d), w_bf16)


def _final_norm_kernel(x_ref, g_ref, o_ref):
    o_ref[...] = _rms(x_ref[...], g_ref[...])


def _final_norm(x, g, tile=TOK_TILE):
    t, d = x.shape
    assert t % tile == 0
    return pl.pallas_call(
        _final_norm_kernel,
        grid=(t // tile,),
        in_specs=[
            pl.BlockSpec((tile, d), lambda i: (i, 0)),
            pl.BlockSpec((1, d), lambda i: (0, 0)),
        ],
        out_specs=pl.BlockSpec((tile, d), lambda i: (i, 0)),
        out_shape=jax.ShapeDtypeStruct((t, d), F32),
        compiler_params=pltpu.CompilerParams(dimension_semantics=("parallel",)),
        name="final_norm",
    )(x, g.reshape(1, d))


def _extract_top(s, out_ref, n):
    for k in range(n):
        m = jnp.max(s, axis=0, keepdims=True)
        out_ref[k:k + 1, :] = m
        s = jnp.where(s == m, -jnp.inf, s)


def _odd_even_merge_sort(n):
    pairs = []
    p = 1
    while p < n:
        k = p
        while k >= 1:
            for j in range(k % p, n - k, 2 * k):
                for i in range(min(k, n - j - k)):
                    if (i + j) // (2 * p) == (i + j + k) // (2 * p):
                        pairs.append((i + j, i + j + k))
            k //= 2
        p *= 2
    return pairs


def _sorted_top(s, out_ref):
    n = s.shape[0] // SUBLANES
    assert s.shape[0] == n * SUBLANES and n & (n - 1) == 0

    def exchange(v, i, j):
        v[i], v[j] = jnp.maximum(v[i], v[j]), jnp.minimum(v[i], v[j])

    v = [s[i * SUBLANES:(i + 1) * SUBLANES, :] for i in range(n)]
    for i, j in _odd_even_merge_sort(n):
        exchange(v, i, j)
    shift = SUBLANES // 2
    while shift >= 1:
        other = [pltpu.roll(t, shift, 0) for t in v]
        v = [jnp.maximum(v[i], other[n - 1 - i]) for i in range(n)]
        stride = n // 2
        while stride >= 1:
            for i in range(n):
                if i & stride == 0:
                    exchange(v, i, i + stride)
            stride //= 2
        shift //= 2
    for k in range(n):
        out_ref[k:k + 1, :] = v[k][0:1, :]


def _peer_kernel(x_ref, g_ref, wqt_ref, sk_ref, u_ref, vt_ref, o_ref,
                 ht_s, q_s, se_s, thr_s, e1r_s, a_s, b_s, cand_s, top_s,
                 sc_s, w_s, acc_s):
    c = pl.program_id(1)
    tok = x_ref.shape[0]
    n_lt = tok // LANES

    @pl.when(c == 0)
    def _route():
        h = _rms(x_ref[...], g_ref[...])
        ht_s[...] = h.T.astype(BF16)
        cand_s[...] = jnp.full(cand_s.shape, -jnp.inf, F32)

        def head_body(hd, carry):
            base = pl.multiple_of(hd * PEER_DK, PEER_DK)
            half = PEER_DK // 2
            row0 = pl.multiple_of(hd * PEER_NKEYS, PEER_NKEYS)
            q_s[...] = jnp.dot(wqt_ref[pl.ds(base, PEER_DK), :], ht_s[...],
                               preferred_element_type=F32).astype(BF16)
            s1 = jnp.dot(sk_ref[0], q_s[0:half, :], preferred_element_type=F32)
            s2 = jnp.dot(sk_ref[1], q_s[half:PEER_DK, :], preferred_element_type=F32)
            for lt in range(n_lt):
                ln = slice(lt * LANES, (lt + 1) * LANES)
                _sorted_top(s1[:, ln], a_s)
                _sorted_top(s2[:, ln], b_s)
                off = 0
                for k, width in enumerate(PEER_STAIR):
                    cand_s[off:off + width, :] = a_s[k:k + 1, :] + b_s[0:width, :]
                    off += width
                _extract_top(cand_s[...], top_s, PEER_TOPK)
                top = top_s[...]
                m0 = top[0:1, :]
                den = jnp.sum(jnp.exp(top - m0), axis=0, keepdims=True)
                tau = top[PEER_TOPK - 1:PEER_TOPK, :]
                s1_t = s1[:, ln]
                acc_s[pl.ds(row0, PEER_NKEYS), ln] = (
                    (tau - s1_t) - PEER_GUARD * (jnp.abs(tau) + jnp.abs(s1_t)))
                sc_s[pl.ds(PEER_HEADS * PEER_NKEYS + row0, PEER_NKEYS), ln] = (
                    (0.5 * jnp.exp(s1[:, ln] - a_s[0:1, :])) / den)
                se_s[lt, hd, 0, 0:PEER_NKEYS, :] = s2[:, ln]
                se_s[lt, hd, 1, 0:PEER_NKEYS, :] = jnp.exp(s2[:, ln] - b_s[0:1, :])
            return carry

        lax.fori_loop(0, PEER_HEADS, head_body, 0)

        for hd in range(PEER_HEADS):
            rows = pl.ds(hd, PEER_NKEYS, stride=PEER_HEADS)
            for lt in range(n_lt):
                ln = slice(lt * LANES, (lt + 1) * LANES)
                thr_s[lt, rows, :] = acc_s[hd * PEER_NKEYS:(hd + 1) * PEER_NKEYS, ln]
                e1r_s[lt, rows, :] = sc_s[(PEER_HEADS + hd) * PEER_NKEYS:
                                          (PEER_HEADS + hd + 1) * PEER_NKEYS, ln]
        acc_s[...] = jnp.zeros_like(acc_s)

    key0 = [sum(PEER_PART_KEYS[:p]) for p in range(len(PEER_PART_KEYS) + 1)]
    part_rows = [slice(key0[p] * PEER_NKEYS, key0[p + 1] * PEER_NKEYS)
                 for p in range(len(PEER_PART_KEYS))]

    def pre_dot(part):
        rows = part_rows[part]
        sc_s[rows, :] = jnp.dot(u_ref[rows, :], ht_s[...], preferred_element_type=F32)

    def build(part):
        for j in range(key0[part], key0[part + 1]):
            r0 = pl.multiple_of((c * PEER_CHUNK_KEYS + j) * PEER_HEADS, PEER_HEADS)
            rows = slice(j * PEER_NKEYS, (j + 1) * PEER_NKEYS)
            for lt in range(n_lt):
                ln = slice(lt * LANES, (lt + 1) * LANES)
                wgt = None
                thr_rows = thr_s[lt, pl.ds(r0, PEER_HEADS), :]
                e1_rows = e1r_s[lt, pl.ds(r0, PEER_HEADS), :]
                for hd in range(PEER_HEADS):
                    s2 = se_s[lt, hd, 0, 0:PEER_NKEYS, :]
                    p = e1_rows[hd:hd + 1, :] * se_s[lt, hd, 1, 0:PEER_NKEYS, :]
                    term = jnp.where(s2 >= thr_rows[hd:hd + 1, :], p, 0.0)
                    wgt = term if wgt is None else wgt + term
                pre = sc_s[rows, ln]
                act = pre * (1.0 + lax.erf(pre * (1.0 / math.sqrt(2.0))))
                w_s[rows, ln] = (wgt * act).astype(BF16)

    def out_dot(part):
        rows = part_rows[part]
        acc_s[...] += jnp.dot(vt_ref[:, rows], w_s[rows, :], preferred_element_type=F32)

    n_parts = len(PEER_PART_KEYS)
    pre_dot(0)
    for part in range(n_parts):
        if part + 1 < n_parts:
            pre_dot(part + 1)
        build(part)
        if part > 0:
            out_dot(part - 1)
    out_dot(n_parts - 1)

    @pl.when(c == pl.num_programs(1) - 1)
    def _finish():
        o_ref[...] = x_ref[...] + acc_s[...].T


def _peer(x, g, wqt_bf16, sk_bf16, u_bf16, vt_bf16, tok=PEER_TOK):
    t, d = x.shape
    assert t % tok == 0 and sum(PEER_PART_KEYS) == PEER_CHUNK_KEYS
    assert PEER_HEADS * PEER_NKEYS <= min(d, PEER_CHUNK // 2)
    n_chunks = PEER_EXPERTS // PEER_CHUNK
    hq = PEER_HEADS * PEER_DK
    return pl.pallas_call(
        _peer_kernel,
        grid=(t // tok, n_chunks),
        in_specs=[
            pl.BlockSpec((tok, d), lambda i, c: (i, 0), pipeline_mode=pl.Buffered(1)),
            pl.BlockSpec((1, d), lambda i, c: (0, 0)),
            pl.BlockSpec((hq, d), lambda i, c: (0, 0), pipeline_mode=pl.Buffered(1)),
            pl.BlockSpec((2, PEER_NKEYS, PEER_DK // 2), lambda i, c: (0, 0, 0)),
            pl.BlockSpec((PEER_CHUNK, d), lambda i, c: (c, 0)),
            pl.BlockSpec((d, PEER_CHUNK), lambda i, c: (0, c)),
        ],
        out_specs=pl.BlockSpec((tok, d), lambda i, c: (i, 0)),
        out_shape=jax.ShapeDtypeStruct((t, d), F32),
        scratch_shapes=[
            pltpu.VMEM((d, tok), BF16),
            pltpu.VMEM((PEER_DK, tok), BF16),
            pltpu.VMEM((tok // LANES, PEER_HEADS, 2, PEER_NKEYS + SUBLANES, LANES), F32),
            pltpu.VMEM((tok // LANES, PEER_NKEYS * PEER_HEADS, LANES), F32),
            pltpu.VMEM((tok // LANES, PEER_NKEYS * PEER_HEADS, LANES), F32),
            pltpu.VMEM((PEER_TOPK, LANES), F32),
            pltpu.VMEM((PEER_TOPK, LANES), F32),
            pltpu.VMEM((PEER_NCAND, LANES), F32),
            pltpu.VMEM((PEER_TOPK, LANES), F32),
            pltpu.VMEM((PEER_CHUNK, tok), F32),
            pltpu.VMEM((PEER_CHUNK, tok), BF16),
            pltpu.VMEM((d, tok), F32),
        ],
        compiler_params=pltpu.CompilerParams(
            dimension_semantics=("parallel", "arbitrary"), vmem_limit_bytes=VMEM_LIMIT),
        name="peer",
    )(x, g.reshape(1, d), wqt_bf16, sk_bf16, u_bf16, vt_bf16)


def _pool_tile_plan(ct):
    lo = ct * LANES
    hi = lo + LANES - 1
    return [(POOL_WINDOWS[g], (g + 1) * POOL_GW) for g in range(lo // POOL_GW, hi // POOL_GW + 1)]


def _pool_diff_tile(load_shifted, z_tile, pos1, ct):
    plan = _pool_tile_plan(ct)
    wanted = {w for w, _ in plan}
    acc = z_tile
    snaps = {}
    for j in range(1, max(wanted)):
        acc = acc + load_shifted(j)
        if j + 1 in wanted:
            snaps[j + 1] = acc
    if len(plan) == 1:
        w = plan[0][0]
        win = snaps[w]
        cnt = jnp.minimum(float(w), pos1)
    else:
        (w_lo, edge), (w_hi, _) = plan
        lane = lax.broadcasted_iota(jnp.int32, (1,) * (z_tile.ndim - 1) + (LANES,), z_tile.ndim - 1)
        in_lo = lane + ct * LANES < edge
        win = jnp.where(in_lo, snaps[w_lo], snaps[w_hi])
        cnt = jnp.minimum(jnp.where(in_lo, float(w_lo), float(w_hi)), pos1)
    return win / cnt - z_tile


def _pool_prompt_kernel(z_ref, wbd_ref, scale_ref, mix_ref, state_ref, zc_s, d_s):
    i = pl.program_id(1)
    tile = z_ref.shape[0]

    @pl.when(i == 0)
    def _start():
        zc_s[0:POOL_HALO, :] = jnp.zeros((POOL_HALO, MIX_W), F32)

    zc_s[POOL_HALO:POOL_HALO + tile, :] = z_ref[...]
    pos1 = (i * tile + lax.broadcasted_iota(jnp.int32, (tile, 1), 0) + 1).astype(F32)
    for ct in range(MIX_W // LANES):
        ln = slice(ct * LANES, (ct + 1) * LANES)
        diff = _pool_diff_tile(
            lambda j: zc_s[POOL_HALO - j:POOL_HALO - j + tile, ln], z_ref[:, ln], pos1, ct)
        d_s[:, ln] = diff.astype(BF16)
    mix_ref[...] = jnp.dot(d_s[...], wbd_ref[...], preferred_element_type=F32) * scale_ref[...]
    last = z_ref[tile - POOL_HALO:tile, :]
    state_ref[...] = last
    zc_s[0:POOL_HALO, :] = last


def _pool_prompt(proj, nb, seq, wbd_bf16, scale, tile=TOK_TILE):
    nt = seq // tile
    assert seq % tile == 0 and tile >= POOL_HALO
    return pl.pallas_call(
        _pool_prompt_kernel,
        grid=(nb, nt),
        in_specs=[
            pl.BlockSpec((tile, MIX_W), lambda n, i: (n * nt + i, 0)),
            pl.BlockSpec((MIX_W, MIX_W), lambda n, i: (0, 0)),
            pl.BlockSpec((1, MIX_W), lambda n, i: (0, 0)),
        ],
        out_specs=[
            pl.BlockSpec((tile, MIX_W), lambda n, i: (n * nt + i, 0)),
            pl.BlockSpec((None, POOL_HALO, MIX_W), lambda n, i: (n, 0, 0)),
        ],
        out_shape=[
            jax.ShapeDtypeStruct((nb * seq, MIX_W), F32),
            jax.ShapeDtypeStruct((nb, POOL_HALO, MIX_W), F32),
        ],
        scratch_shapes=[
            pltpu.VMEM((POOL_HALO + tile, MIX_W), F32),
            pltpu.VMEM((tile, MIX_W), BF16),
        ],
        compiler_params=pltpu.CompilerParams(
            dimension_semantics=("arbitrary", "arbitrary"), vmem_limit_bytes=VMEM_LIMIT),
        name="pool_prompt",
    )(proj, wbd_bf16, scale.reshape(1, MIX_W))


def _pool_sample_kernel(z_ref, st_ref, wbd_ref, scale_ref, mix_ref, nst_ref, zc_s, d_s, *, pos0):
    grp, n_state, _ = st_ref.shape
    t = z_ref.shape[0] // grp
    zc_s[:, POOL_HALO - n_state:POOL_HALO, :] = st_ref[...]
    zc_s[:, POOL_HALO:POOL_HALO + t, :] = z_ref[...].reshape(grp, t, MIX_W)
    pos1 = (pos0 + lax.broadcasted_iota(jnp.int32, (1, t, 1), 1) + 1).astype(F32)
    for ct in range(MIX_W // LANES):
        ln = slice(ct * LANES, (ct + 1) * LANES)
        diff = _pool_diff_tile(
            lambda j: zc_s[:, POOL_HALO - j:POOL_HALO - j + t, ln],
            zc_s[:, POOL_HALO:POOL_HALO + t, ln], pos1, ct)
        d_s[:, ln] = diff.reshape(grp * t, LANES).astype(BF16)
    mix_ref[...] = jnp.dot(d_s[...], wbd_ref[...], preferred_element_type=F32) * scale_ref[...]
    nst_ref[...] = zc_s[:, POOL_HALO + t - n_state:POOL_HALO + t, :]


def _pool_sample(proj, nb, t, state, wbd_bf16, scale, pos0, grp=SAMPLE_GROUP):
    n_state = state.shape[1]
    assert nb % grp == 0 and t % SUBLANES == 0
    assert max(POOL_WINDOWS) - 1 <= n_state <= POOL_HALO - 1
    return pl.pallas_call(
        functools.partial(_pool_sample_kernel, pos0=pos0),
        grid=(nb // grp,),
        in_specs=[
            pl.BlockSpec((grp * t, MIX_W), lambda i: (i, 0)),
            pl.BlockSpec((grp, n_state, MIX_W), lambda i: (i, 0, 0)),
            pl.BlockSpec((MIX_W, MIX_W), lambda i: (0, 0)),
            pl.BlockSpec((1, MIX_W), lambda i: (0, 0)),
        ],
        out_specs=[
            pl.BlockSpec((grp * t, MIX_W), lambda i: (i, 0)),
            pl.BlockSpec((grp, n_state, MIX_W), lambda i: (i, 0, 0)),
        ],
        out_shape=[
            jax.ShapeDtypeStruct((nb * t, MIX_W), F32),
            jax.ShapeDtypeStruct((nb, n_state, MIX_W), F32),
        ],
        scratch_shapes=[
            pltpu.VMEM((grp, POOL_HALO + t, MIX_W), F32),
            pltpu.VMEM((grp * t, MIX_W), BF16),
        ],
        compiler_params=pltpu.CompilerParams(
            dimension_semantics=("parallel",), vmem_limit_bytes=VMEM_LIMIT),
        name="pool_sample",
    )(proj, state, wbd_bf16, scale.reshape(1, MIX_W))


def _mem_out_kernel(x_ref, mix_ref, qm_ref, mk_ref, mv_ref, wmix_ref, wmem_ref, o_ref):
    grp = mk_ref.shape[0]
    rows = x_ref.shape[0]
    tq = rows // grp
    acc = x_ref[...] + jnp.dot(mix_ref[...].astype(BF16), wmix_ref[...],
                               preferred_element_type=F32)
    for h in range(MEM_HEADS):
        cols = slice(h * MEM_HD, (h + 1) * MEM_HD)
        q = qm_ref[:, cols].astype(BF16).reshape(grp, tq, MEM_HD)
        k = mk_ref[:, :, cols].astype(BF16)
        v = mv_ref[:, :, cols].astype(BF16)
        s = jnp.einsum('bqd,bkd->bqk', q, k, preferred_element_type=F32) * (MEM_HD ** -0.5)
        e = jnp.exp(s - jnp.max(s, axis=-1, keepdims=True))
        p = e / jnp.sum(e, axis=-1, keepdims=True)
        o = jnp.einsum('bqk,bkd->bqd', p.astype(BF16), v, preferred_element_type=F32)
        acc = acc + jnp.dot(o.reshape(rows, MEM_HD).astype(BF16), wmem_ref[cols, :],
                            preferred_element_type=F32)
    o_ref[...] = acc


def _mem_out(x, mix, proj, qm_col_blk, mem_k, mem_v, wmix_bf16, wmem_bf16, grp, tq):
    rows, d = x.shape
    kmix = mix.shape[1]
    n_seq, n_mem, _ = mem_k.shape
    step = grp * tq
    per_seq = rows // n_seq
    assert rows % step == 0 and tq % SUBLANES == 0
    assert (grp == 1 and per_seq % tq == 0) or per_seq == tq
    tiles_per_seq = per_seq // tq
    seq_blk = (lambda i: i // tiles_per_seq) if grp == 1 else (lambda i: i)
    return pl.pallas_call(
        _mem_out_kernel,
        grid=(rows // step,),
        in_specs=[
            pl.BlockSpec((step, d), lambda i: (i, 0)),
            pl.BlockSpec((step, kmix), lambda i: (i, 0)),
            pl.BlockSpec((step, MEM_W), lambda i: (i, qm_col_blk)),
            pl.BlockSpec((grp, n_mem, MEM_W), lambda i: (seq_blk(i), 0, 0)),
            pl.BlockSpec((grp, n_mem, MEM_W), lambda i: (seq_blk(i), 0, 0)),
            pl.BlockSpec((kmix, d), lambda i: (0, 0)),
            pl.BlockSpec((MEM_W, d), lambda i: (0, 0)),
        ],
        out_specs=pl.BlockSpec((step, d), lambda i: (i, 0)),
        out_shape=jax.ShapeDtypeStruct((rows, d), F32),
        compiler_params=pltpu.CompilerParams(
            dimension_semantics=("parallel",), vmem_limit_bytes=VMEM_LIMIT),
        name="mem_out",
    )(x, mix, proj, mem_k, mem_v, wmix_bf16, wmem_bf16)


def _dil_prompt_kernel(slope_ref, q_ref, kc_ref, kp_ref, vc_ref, vp_ref, o_ref, og_s, lse_s):
    win = pl.program_id(1)
    rows = q_ref.shape[0]
    blk = DIL_BLOCK
    slope = slope_ref[...]
    ri = lax.broadcasted_iota(jnp.int32, (blk, blk), 0)
    ci = lax.broadcasted_iota(jnp.int32, (blk, blk), 1)
    diff = (ri - ci).astype(F32)
    own_ok = (ri >= ci)[None]
    prev_ok = (ci >= ri)[None]
    not_first = win > 0
    scale = B_HD ** -0.5

    for g, (w, d) in enumerate(DIL_PATTERNS):
        per_res = rows // (blk * d)
        assert w // d == blk and rows % (blk * d) == 0 and per_res & (per_res - 1) == 0

        def rows_of(r, j):
            return pl.ds(j * blk * d + r, blk, stride=d) if d > 1 else pl.ds(j * blk, blk)

        own = [rows_of(r, j) for r in range(d) for j in range(per_res)]
        prv = [(kc_ref, vc_ref, rows_of(r, j - 1)) if j > 0
               else (kp_ref, vp_ref, rows_of(r, per_res - 1))
               for r in range(d) for j in range(per_res)]
        q = jnp.stack([q_ref[o, :] for o in own]).astype(BF16)
        k_own = jnp.stack([kc_ref[o, :] for o in own]).astype(BF16)
        v_own = jnp.stack([vc_ref[o, :] for o in own]).astype(BF16)
        k_prev = jnp.stack([kr[o, :] for kr, _, o in prv]).astype(BF16)
        v_prev = jnp.stack([vr[o, :] for _, vr, o in prv]).astype(BF16)
        bias_own = (-(slope * float(d)) * diff)[None]
        bias_prev = (-(slope * float(d)) * (diff + float(blk)))[None]
        s_own = jnp.einsum('bqd,bkd->bqk', q, k_own, preferred_element_type=F32) * scale + bias_own
        s_own = jnp.where(own_ok, s_own, NEG)
        s_prev = jnp.einsum('bqd,bkd->bqk', q, k_prev, preferred_element_type=F32) * scale + bias_prev
        bi = lax.broadcasted_iota(jnp.int32, (len(own), blk, blk), 0)
        has_prev = jnp.logical_or((bi & (per_res - 1)) != 0, not_first)
        s_prev = jnp.where(jnp.logical_and(prev_ok, has_prev), s_prev, NEG)
        m = jnp.maximum(jnp.max(s_own, axis=-1, keepdims=True),
                        jnp.max(s_prev, axis=-1, keepdims=True))
        e_own = jnp.exp(s_own - m)
        e_prev = jnp.exp(s_prev - m)
        den = jnp.sum(e_own, axis=-1, keepdims=True) + jnp.sum(e_prev, axis=-1, keepdims=True)
        o = (jnp.einsum('bqk,bkd->bqd', (e_own / den).astype(BF16), v_own,
                        preferred_element_type=F32)
             + jnp.einsum('bqk,bkd->bqd', (e_prev / den).astype(BF16), v_prev,
                          preferred_element_type=F32))
        lse = jnp.broadcast_to(m + jnp.log(den), o.shape)
        for b, rows_b in enumerate(own):
            og_s[g, rows_b, :] = o[b]
            lse_s[g, rows_b, :] = lse[b]

    top = jnp.maximum(jnp.maximum(lse_s[0], lse_s[1]), lse_s[2])
    wts = [jnp.exp(lse_s[g] - top) for g in range(len(DIL_PATTERNS))]
    tot = wts[0] + wts[1] + wts[2]
    o_ref[...] = ((wts[0] / tot) * og_s[0] + (wts[1] / tot) * og_s[1] + (wts[2] / tot) * og_s[2])


def _dil_prompt(proj_pad, kv_pad, nb, seq, slopes_b):
    win = DIL_WIN
    nw = seq // win
    assert seq % win == 0 and len(DIL_PATTERNS) == 3
    cur = lambda col0: (lambda n, w, h: (n * nw + w, col0 + h))
    prev = lambda col0: (lambda n, w, h: (n * nw + jnp.maximum(w - 1, 0), col0 + h))
    return pl.pallas_call(
        _dil_prompt_kernel,
        grid=(nb, nw, B_HEADS),
        in_specs=[
            pl.BlockSpec((None, 1, LANES), lambda n, w, h: (h, 0, 0)),
            pl.BlockSpec((win, LANES), cur(0)),
            pl.BlockSpec((win, LANES), cur(0)),
            pl.BlockSpec((win, LANES), prev(0)),
            pl.BlockSpec((win, LANES), cur(B_HEADS)),
            pl.BlockSpec((win, LANES), prev(B_HEADS)),
        ],
        out_specs=pl.BlockSpec((win, LANES), cur(0)),
        out_shape=jax.ShapeDtypeStruct((nb * seq, B_HEADS * LANES), F32),
        scratch_shapes=[
            pltpu.VMEM((len(DIL_PATTERNS), win, LANES), F32),
            pltpu.VMEM((len(DIL_PATTERNS), win, LANES), F32),
        ],
        compiler_params=pltpu.CompilerParams(
            dimension_semantics=("parallel", "parallel", "parallel"),
            vmem_limit_bytes=VMEM_LIMIT),
        name="dil_prompt",
    )(slopes_b, proj_pad, kv_pad, kv_pad, kv_pad, kv_pad)


def _dil_sample_kernel(q_ref, kc_ref, vc_ref, kn_ref, vn_ref, bc_ref, bn_ref, hm_ref, o_ref,
                       knp_s, vnp_s, pc_s, *, row_lo):
    t = q_ref.shape[0]
    nt = (((1,), (1,)), ((), ()))
    scale = B_HD ** -0.5
    hm = hm_ref[...]
    qbd = (jnp.concatenate([q_ref[...]] * B_HEADS, axis=0) * hm).astype(BF16)
    knp_s[...] = jnp.zeros_like(knp_s)
    vnp_s[...] = jnp.zeros_like(vnp_s)
    knp_s[0:t, :] = kn_ref[...]
    vnp_s[0:t, :] = vn_ref[...]
    s_c = lax.dot_general(qbd, kc_ref[...].astype(BF16), nt, preferred_element_type=F32) * scale
    s_n = lax.dot_general(qbd, knp_s[...].astype(BF16), nt, preferred_element_type=F32) * scale
    stats = []
    for g in range(len(DIL_PATTERNS)):
        lo = row_lo[g]
        bc = bc_ref[g, :, lo:]
        bn = bn_ref[g]
        sc = jnp.where(bc > 0.5 * NEG, s_c[:, lo:] + bc, NEG)
        sn = jnp.where(bn > 0.5 * NEG, s_n + bn, NEG)
        m = jnp.maximum(jnp.max(sc, axis=-1, keepdims=True), jnp.max(sn, axis=-1, keepdims=True))
        ec = jnp.exp(sc - m)
        en = jnp.exp(sn - m)
        den = jnp.sum(ec, axis=-1, keepdims=True) + jnp.sum(en, axis=-1, keepdims=True)
        stats.append((ec / den, en / den, m + jnp.log(den)))
    top = jnp.maximum(jnp.maximum(stats[0][2], stats[1][2]), stats[2][2])
    wts = [jnp.exp(st[2] - top) for st in stats]
    tot = wts[0] + wts[1] + wts[2]
    pc_s[...] = jnp.zeros_like(pc_s)
    pn = jnp.zeros((B_HEADS * t, knp_s.shape[0]), F32)
    for g in range(len(DIL_PATTERNS)):
        lo = row_lo[g]
        alpha = wts[g] / tot
        pc_s[:, lo:] += alpha * stats[g][0]
        pn = pn + alpha * stats[g][1]
    o = (jnp.dot(pc_s[...].astype(BF16), vc_ref[...].astype(BF16), preferred_element_type=F32)
         + jnp.dot(pn.astype(BF16), vnp_s[...].astype(BF16), preferred_element_type=F32))
    o_ref[...] = jnp.sum((o * hm).reshape(B_HEADS, t, MIX_W), axis=0)


def _dil_sample_tables(past, t):
    slopes = np.asarray([2.0 ** (-8.0 * (h + 1) / B_HEADS) for h in range(B_HEADS)], np.float32)
    qpos = past + np.arange(t)
    col = np.arange(past + LANES)
    exists = col < past + t
    delta = qpos[:, None] - col[None, :]
    bias = np.full((len(DIL_PATTERNS), B_HEADS, t, past + LANES), NEG, np.float32)
    col_lo = []
    for g, (w, d) in enumerate(DIL_PATTERNS):
        ok = (delta >= 0) & (delta <= w) & (delta % d == 0) & exists[None, :]
        vals = -slopes[:, None, None] * delta[None].astype(np.float32)
        bias[g] = np.where(ok[None], vals, np.float32(NEG))
        col_lo.append(max(0, (past - w) // LANES * LANES))
    bias = bias.reshape(len(DIL_PATTERNS), B_HEADS * t, past + LANES)
    hm = (np.arange(MIX_W)[None, :] // B_HD == np.arange(B_HEADS * t)[:, None] // t)
    return bias[:, :, :past], bias[:, :, past:], hm.astype(np.float32), tuple(col_lo)


def _dil_sample(proj, kv_new, cache_k, cache_v):
    nb, past, _ = cache_k.shape
    t = proj.shape[0] // nb
    assert t % SUBLANES == 0 and t <= LANES and past % LANES == 0 and len(DIL_PATTERNS) == 3
    bias_c, bias_n, hm, col_lo = _dil_sample_tables(past, t)
    ng = len(DIL_PATTERNS)
    return pl.pallas_call(
        functools.partial(_dil_sample_kernel, row_lo=col_lo),
        grid=(nb,),
        in_specs=[
            pl.BlockSpec((t, MIX_W), lambda b: (b, 0)),
            pl.BlockSpec((None, past, MIX_W), lambda b: (b, 0, 0)),
            pl.BlockSpec((None, past, MIX_W), lambda b: (b, 0, 0)),
            pl.BlockSpec((t, MIX_W), lambda b: (b, 0)),
            pl.BlockSpec((t, MIX_W), lambda b: (b, 1)),
            pl.BlockSpec((ng, B_HEADS * t, past), lambda b: (0, 0, 0)),
            pl.BlockSpec((ng, B_HEADS * t, LANES), lambda b: (0, 0, 0)),
            pl.BlockSpec((B_HEADS * t, MIX_W), lambda b: (0, 0)),
        ],
        out_specs=pl.BlockSpec((t, MIX_W), lambda b: (b, 0)),
        out_shape=jax.ShapeDtypeStruct((nb * t, MIX_W), F32),
        scratch_shapes=[
            pltpu.VMEM((LANES, MIX_W), F32),
            pltpu.VMEM((LANES, MIX_W), F32),
            pltpu.VMEM((B_HEADS * t, past), F32),
        ],
        compiler_params=pltpu.CompilerParams(
            dimension_semantics=("parallel",), vmem_limit_bytes=VMEM_LIMIT),
        name="dil_sample",
    )(proj, cache_k, cache_v, kv_new, kv_new, jnp.asarray(bias_c), jnp.asarray(bias_n),
      jnp.asarray(hm))


def _pad_head_cols(w):
    d = w.shape[0]
    w = jnp.pad(w.reshape(d, B_HEADS, B_HD), ((0, 0), (0, 0), (0, LANES - B_HD)))
    return w.reshape(d, B_HEADS * LANES)


def _unpad_heads(a):
    return a.reshape(a.shape[:-1] + (B_HEADS, LANES))[..., :B_HD]


def kernel(x_prompt, x_sample, state_pool, cache_win_k, cache_win_v, cache_mem_k, cache_mem_v,
           mem_prompt, norm_mix, w_in, pool_w, pool_scale, norm_mem, w_mem_kv, w_out,
           norm_kv, w_kv, norm_ffn, peer_wq, peer_subkeys, peer_u, peer_v, norm_final):
    nb_p, seq, d = x_prompt.shape
    nb_s, dec_seq, _ = x_sample.shape
    n_mem = mem_prompt.shape[1]
    past = cache_win_k.shape[1]
    tp = nb_p * seq
    ts = nb_s * dec_seq

    mem_flat = mem_prompt.reshape(nb_p * n_mem, d)
    mkv = [_norm_matmul(mem_flat, norm_mem[l], w_mem_kv[l].astype(BF16)) for l in range(DEPTH)]
    mem_k_p = jnp.stack([m[:, :MEM_W].reshape(nb_p, n_mem, MEM_W) for m in mkv], axis=0)
    mem_v_p = jnp.stack([m[:, MEM_W:].reshape(nb_p, n_mem, MEM_W) for m in mkv], axis=0)
    mem_k_s = cache_mem_k.reshape(DEPTH, nb_s, n_mem, MEM_W)
    mem_v_s = cache_mem_v.reshape(DEPTH, nb_s, n_mem, MEM_W)

    slopes_b = jnp.broadcast_to(
        jnp.asarray([2.0 ** (-8.0 * (h + 1) / B_HEADS) for h in range(B_HEADS)], F32)[:, None, None],
        (B_HEADS, 1, LANES))
    cache_k = cache_win_k.reshape(nb_s, past, MIX_W)
    cache_v = cache_win_v.reshape(nb_s, past, MIX_W)

    xp = x_prompt.reshape(tp, d)
    xs = x_sample.reshape(ts, d)
    pool_p, pool_s = [], []
    kv_p = kv_s = None
    for l in range(DEPTH):
        w_in_l = w_in[l].astype(BF16)
        w_out_l = w_out[l].astype(BF16)
        proj_s = _norm_matmul(xs, norm_mix[l], w_in_l)
        if l < N_A:
            proj_p = _norm_matmul(xp, norm_mix[l], w_in_l)
            wbd = jax.scipy.linalg.block_diag(*[pool_w[l, g] for g in range(POOL_GROUPS)]).astype(BF16)
            mix_p, last_p = _pool_prompt(proj_p, nb_p, seq, wbd, pool_scale[l])
            pool_p.append(last_p[:, POOL_HALO - POOL_STATE:])
            mix_s, new_state = _pool_sample(proj_s, nb_s, dec_seq, state_pool[l], wbd,
                                            pool_scale[l], PAST_LEN)
            pool_s.append(new_state)
            wmix_p = w_out_l[:MIX_W]
            qm_blk_p = MIX_W // MEM_W
        else:
            w_in_pad = jnp.concatenate([_pad_head_cols(w_in[l][:, :MIX_W]), w_in[l][:, MIX_W:]],
                                       axis=1).astype(BF16)
            proj_p = _norm_matmul(xp, norm_mix[l], w_in_pad)
            mix_p = _dil_prompt(proj_p, kv_p, nb_p, seq, slopes_b)
            mix_s = _dil_sample(proj_s, kv_s, cache_k, cache_v)
            wmix_p = _pad_head_cols(w_out[l][:MIX_W].T).T.astype(BF16)
            qm_blk_p = B_HEADS * LANES // MEM_W
        xp = _mem_out(xp, mix_p, proj_p, qm_blk_p, mem_k_p[l], mem_v_p[l], wmix_p,
                      w_out_l[MIX_W:], 1, TOK_TILE)
        xs = _mem_out(xs, mix_s, proj_s, MIX_W // MEM_W, mem_k_s[l], mem_v_s[l], w_out_l[:MIX_W],
                      w_out_l[MIX_W:], SAMPLE_GROUP, dec_seq)
        peer_w = (norm_ffn[l], peer_wq[l].T.astype(BF16), peer_subkeys[l].astype(BF16),
                  peer_u[l].astype(BF16), peer_v[l].T.astype(BF16))
        xp = _peer(xp, *peer_w)
        xs = _peer(xs, *peer_w)
        if l == N_A - 1:
            w_kv_pad = jnp.concatenate([_pad_head_cols(w_kv[:, :MIX_W]),
                                        _pad_head_cols(w_kv[:, MIX_W:])], axis=1).astype(BF16)
            kv_p = _norm_matmul(xp, norm_kv, w_kv_pad)
            kv_s = _norm_matmul(xs, norm_kv, w_kv.astype(BF16))
    y_p = _final_norm(xp, norm_final)
    y_s = _final_norm(xs, norm_final)
    keep = min(WINDOW_MAX, seq)
    kv_p4 = kv_p.reshape(nb_p, seq, 2 * B_HEADS * LANES)[:, seq - keep:]
    return (y_p.reshape(nb_p, seq, d), y_s.reshape(nb_s, dec_seq, d),
            jnp.stack(pool_p, axis=0),
            _unpad_heads(kv_p4[..., :B_HEADS * LANES]), _unpad_heads(kv_p4[..., B_HEADS * LANES:]),
            mem_k_p.reshape(DEPTH, nb_p, n_mem, MEM_HEADS, MEM_HD),
            mem_v_p.reshape(DEPTH, nb_p, n_mem, MEM_HEADS, MEM_HD),
            jnp.stack(pool_s, axis=0),
            kv_s[:, :MIX_W].reshape(nb_s, dec_seq, B_HEADS, B_HD),
            kv_s[:, MIX_W:].reshape(nb_s, dec_seq, B_HEADS, B_HD))
```

```python
import functools
import math

import numpy as np
import jax
import jax.numpy as jnp
from jax import lax
from jax.experimental import pallas as pl
from jax.experimental.pallas import tpu as pltpu

F32 = jnp.float32
BF16 = jnp.bfloat16

D_MODEL = 1024
DEPTH = 4
N_A = DEPTH // 2
MIX_W = 3 * D_MODEL // 4
MEM_HEADS = 4
MEM_HD = (D_MODEL - MIX_W) // MEM_HEADS
MEM_W = MEM_HEADS * MEM_HD
POOL_WINDOWS = (2, 4, 8, 16)
POOL_GROUPS = len(POOL_WINDOWS)
POOL_GW = MIX_W // POOL_GROUPS
POOL_STATE = max(POOL_WINDOWS) - 1
B_HEADS = 8
B_HD = MIX_W // B_HEADS
DIL_PATTERNS = ((128, 1), (512, 4), (2048, 16))
WINDOW_MAX = max(w for w, _ in DIL_PATTERNS)
PEER_HEADS = 8
PEER_NKEYS = 128
PEER_EXPERTS = PEER_NKEYS * PEER_NKEYS
PEER_DK = 256
PEER_TOPK = 16
PAST_LEN = 2048
EPS = 1e-6
NEG = -1e30

LANES = 128
SUBLANES = 8
TOK_TILE = 512
PEER_TOK = 512
PEER_CHUNK_KEYS = 16
PEER_CHUNK = PEER_CHUNK_KEYS * PEER_NKEYS
PEER_PART_KEYS = (8, 8)
PEER_STAIR = tuple(PEER_TOPK // (k + 1) for k in range(PEER_TOPK))
PEER_NCAND = -(-sum(PEER_STAIR) // SUBLANES) * SUBLANES
PEER_GUARD = 2.0 ** -21
POOL_HALO = 16
DIL_BLOCK = 128
DIL_WIN = 2048
SAMPLE_GROUP = 16
VMEM_LIMIT = 56 * 1024 * 1024


def _rms(x, g):
    r = lax.rsqrt(jnp.mean(x * x, axis=-1, keepdims=True) + EPS)
    return (x * r) * g


def _norm_matmul_kernel(x_ref, g_ref, w_ref, o_ref):
    h = _rms(x_ref[...], g_ref[...])
    o_ref[...] = jnp.dot(h.astype(BF16), w_ref[...], preferred_element_type=F32)


def _norm_matmul(x, g, w_bf16, tile=TOK_TILE):
    t, d = x.shape
    n = w_bf16.shape[1]
    tile = min(tile, t)
    assert t % tile == 0
    return pl.pallas_call(
        _norm_matmul_kernel,
        grid=(t // tile,),
        in_specs=[
            pl.BlockSpec((tile, d), lambda i: (i, 0)),
            pl.BlockSpec((1, d), lambda i: (0, 0)),
            pl.BlockSpec((d, n), lambda i: (0, 0)),
        ],
        out_specs=pl.BlockSpec((tile, n), lambda i: (i, 0)),
        out_shape=jax.ShapeDtypeStruct((t, n), F32),
        compiler_params=pltpu.CompilerParams(
            dimension_semantics=("parallel",), vmem_limit_bytes=VMEM_LIMIT),
        name="norm_matmul",
    )(x, g.reshape(1, d), w_bf16)


def _final_norm_kernel(x_ref, g_ref, o_ref):
    o_ref[...] = _rms(x_ref[...], g_ref[...])


def _final_norm(x, g, tile=TOK_TILE):
    t, d = x.shape
    assert t % tile == 0
    return pl.pallas_call(
        _final_norm_kernel,
        grid=(t // tile,),
        in_specs=[
            pl.BlockSpec((tile, d), lambda i: (i, 0)),
            pl.BlockSpec((1, d), lambda i: (0, 0)),
        ],
        out_specs=pl.BlockSpec((tile, d), lambda i: (i, 0)),
        out_shape=jax.ShapeDtypeStruct((t, d), F32),
        compiler_params=pltpu.CompilerParams(dimension_semantics=("parallel",)),
        name="final_norm",
    )(x, g.reshape(1, d))


def _extract_top(s, out_ref, n):
    for k in range(n):
        m = jnp.max(s, axis=0, keepdims=True)
        out_ref[k:k + 1, :] = m
        s = jnp.where(s == m, -jnp.inf, s)


def _odd_even_merge_sort(n):
    pairs = []
    p = 1
    while p < n:
        k = p
        while k >= 1:
            for j in range(k % p, n - k, 2 * k):
                for i in range(min(k, n - j - k)):
                    if (i + j) // (2 * p) == (i + j + k) // (2 * p):
                        pairs.append((i + j, i + j + k))
            k //= 2
        p *= 2
    return pairs


def _sorted_top(s, out_ref):
    n = s.shape[0] // SUBLANES
    assert s.shape[0] == n * SUBLANES and n & (n - 1) == 0

    def exchange(v, i, j):
        v[i], v[j] = jnp.maximum(v[i], v[j]), jnp.minimum(v[i], v[j])

    v = [s[i * SUBLANES:(i + 1) * SUBLANES, :] for i in range(n)]
    for i, j in _odd_even_merge_sort(n):
        exchange(v, i, j)
    shift = SUBLANES // 2
    while shift >= 1:
        other = [pltpu.roll(t, shift, 0) for t in v]
        v = [jnp.maximum(v[i], other[n - 1 - i]) for i in range(n)]
        stride = n // 2
        while stride >= 1:
            for i in range(n):
                if i & stride == 0:
                    exchange(v, i, i + stride)
            stride //= 2
        shift //= 2
    for k in range(n):
        out_ref[k:k + 1, :] = v[k][0:1, :]


def _peer_kernel(x_ref, g_ref, wqt_ref, sk_ref, u_ref, vt_ref, o_ref,
                 ht_s, q_s, se_s, thr_s, e1r_s, a_s, b_s, cand_s, top_s,
                 sc_s, w_s, acc_s):
    c = pl.program_id(1)
    tok = x_ref.shape[0]
    n_lt = tok // LANES

    @pl.when(c == 0)
    def _route():
        h = _rms(x_ref[...], g_ref[...])
        ht_s[...] = h.T.astype(BF16)
        cand_s[...] = jnp.full(cand_s.shape, -jnp.inf, F32)

        def head_body(hd, carry):
            base = pl.multiple_of(hd * PEER_DK, PEER_DK)
            half = PEER_DK // 2
            row0 = pl.multiple_of(hd * PEER_NKEYS, PEER_NKEYS)
            q_s[...] = jnp.dot(wqt_ref[pl.ds(base, PEER_DK), :], ht_s[...],
                               preferred_element_type=F32).astype(BF16)
            s1 = jnp.dot(sk_ref[0], q_s[0:half, :], preferred_element_type=F32)
            s2 = jnp.dot(sk_ref[1], q_s[half:PEER_DK, :], preferred_element_type=F32)
            for lt in range(n_lt):
                ln = slice(lt * LANES, (lt + 1) * LANES)
                _sorted_top(s1[:, ln], a_s)
                _sorted_top(s2[:, ln], b_s)
                off = 0
                for k, width in enumerate(PEER_STAIR):
                    cand_s[off:off + width, :] = a_s[k:k + 1, :] + b_s[0:width, :]
                    off += width
                _extract_top(cand_s[...], top_s, PEER_TOPK)
                top = top_s[...]
                m0 = top[0:1, :]
                den = jnp.sum(jnp.exp(top - m0), axis=0, keepdims=True)
                tau = top[PEER_TOPK - 1:PEER_TOPK, :]
                s1_t = s1[:, ln]
                acc_s[pl.ds(row0, PEER_NKEYS), ln] = (
                    (tau - s1_t) - PEER_GUARD * (jnp.abs(tau) + jnp.abs(s1_t)))
                sc_s[pl.ds(PEER_HEADS * PEER_NKEYS + row0, PEER_NKEYS), ln] = (
                    (0.5 * jnp.exp(s1[:, ln] - a_s[0:1, :])) / den)
                se_s[lt, hd, 0, 0:PEER_NKEYS, :] = s2[:, ln]
                se_s[lt, hd, 1, 0:PEER_NKEYS, :] = jnp.exp(s2[:, ln] - b_s[0:1, :])
            return carry

        lax.fori_loop(0, PEER_HEADS, head_body, 0)

        for hd in range(PEER_HEADS):
            rows = pl.ds(hd, PEER_NKEYS, stride=PEER_HEADS)
            for lt in range(n_lt):
                ln = slice(lt * LANES, (lt + 1) * LANES)
                thr_s[lt, rows, :] = acc_s[hd * PEER_NKEYS:(hd + 1) * PEER_NKEYS, ln]
                e1r_s[lt, rows, :] = sc_s[(PEER_HEADS + hd) * PEER_NKEYS:
                                          (PEER_HEADS + hd + 1) * PEER_NKEYS, ln]
        acc_s[...] = jnp.zeros_like(acc_s)

    key0 = [sum(PEER_PART_KEYS[:p]) for p in range(len(PEER_PART_KEYS) + 1)]
    part_rows = [slice(key0[p] * PEER_NKEYS, key0[p + 1] * PEER_NKEYS)
                 for p in range(len(PEER_PART_KEYS))]

    def pre_dot(part):
        rows = part_rows[part]
        sc_s[rows, :] = jnp.dot(u_ref[rows, :], ht_s[...], preferred_element_type=F32)

    def build(part):
        for j in range(key0[part], key0[part + 1]):
            r0 = pl.multiple_of((c * PEER_CHUNK_KEYS + j) * PEER_HEADS, PEER_HEADS)
            rows = slice(j * PEER_NKEYS, (j + 1) * PEER_NKEYS)
            for lt in range(n_lt):
                ln = slice(lt * LANES, (lt + 1) * LANES)
                wgt = None
                thr_rows = thr_s[lt, pl.ds(r0, PEER_HEADS), :]
                e1_rows = e1r_s[lt, pl.ds(r0, PEER_HEADS), :]
                for hd in range(PEER_HEADS):
                    s2 = se_s[lt, hd, 0, 0:PEER_NKEYS, :]
                    p = e1_rows[hd:hd + 1, :] * se_s[lt, hd, 1, 0:PEER_NKEYS, :]
                    term = jnp.where(s2 >= thr_rows[hd:hd + 1, :], p, 0.0)
                    wgt = term if wgt is None else wgt + term
                pre = sc_s[rows, ln]
                act = pre * (1.0 + lax.erf(pre * (1.0 / math.sqrt(2.0))))
                w_s[rows, ln] = (wgt * act).astype(BF16)

    def out_dot(part):
        rows = part_rows[part]
        acc_s[...] += jnp.dot(vt_ref[:, rows], w_s[rows, :], preferred_element_type=F32)

    n_parts = len(PEER_PART_KEYS)
    pre_dot(0)
    for part in range(n_parts):
        if part + 1 < n_parts:
            pre_dot(part + 1)
        build(part)
        if part > 0:
            out_dot(part - 1)
    out_dot(n_parts - 1)

    @pl.when(c == pl.num_programs(1) - 1)
    def _finish():
        o_ref[...] = x_ref[...] + acc_s[...].T


def _peer(x, g, wqt_bf16, sk_bf16, u_bf16, vt_bf16, tok=PEER_TOK):
    t, d = x.shape
    assert t % tok == 0 and sum(PEER_PART_KEYS) == PEER_CHUNK_KEYS
    assert PEER_HEADS * PEER_NKEYS <= min(d, PEER_CHUNK // 2)
    n_chunks = PEER_EXPERTS // PEER_CHUNK
    hq = PEER_HEADS * PEER_DK
    return pl.pallas_call(
        _peer_kernel,
        grid=(t // tok, n_chunks),
        in_specs=[
            pl.BlockSpec((tok, d), lambda i, c: (i, 0), pipeline_mode=pl.Buffered(1)),
            pl.BlockSpec((1, d), lambda i, c: (0, 0)),
            pl.BlockSpec((hq, d), lambda i, c: (0, 0), pipeline_mode=pl.Buffered(1)),
            pl.BlockSpec((2, PEER_NKEYS, PEER_DK // 2), lambda i, c: (0, 0, 0)),
            pl.BlockSpec((PEER_CHUNK, d), lambda i, c: (c, 0)),
            pl.BlockSpec((d, PEER_CHUNK), lambda i, c: (0, c)),
        ],
        out_specs=pl.BlockSpec((tok, d), lambda i, c: (i, 0)),
        out_shape=jax.ShapeDtypeStruct((t, d), F32),
        scratch_shapes=[
            pltpu.VMEM((d, tok), BF16),
            pltpu.VMEM((PEER_DK, tok), BF16),
            pltpu.VMEM((tok // LANES, PEER_HEADS, 2, PEER_NKEYS + SUBLANES, LANES), F32),
            pltpu.VMEM((tok // LANES, PEER_NKEYS * PEER_HEADS, LANES), F32),
            pltpu.VMEM((tok // LANES, PEER_NKEYS * PEER_HEADS, LANES), F32),
            pltpu.VMEM((PEER_TOPK, LANES), F32),
            pltpu.VMEM((PEER_TOPK, LANES), F32),
            pltpu.VMEM((PEER_NCAND, LANES), F32),
            pltpu.VMEM((PEER_TOPK, LANES), F32),
            pltpu.VMEM((PEER_CHUNK, tok), F32),
            pltpu.VMEM((PEER_CHUNK, tok), BF16),
            pltpu.VMEM((d, tok), F32),
        ],
        compiler_params=pltpu.CompilerParams(
            dimension_semantics=("parallel", "arbitrary"), vmem_limit_bytes=VMEM_LIMIT),
        name="peer",
    )(x, g.reshape(1, d), wqt_bf16, sk_bf16, u_bf16, vt_bf16)


def _pool_tile_plan(ct):
    lo = ct * LANES
    hi = lo + LANES - 1
    return [(POOL_WINDOWS[g], (g + 1) * POOL_GW) for g in range(lo // POOL_GW, hi // POOL_GW + 1)]


def _pool_diff_tile(load_shifted, z_tile, pos1, ct):
    plan = _pool_tile_plan(ct)
    wanted = {w for w, _ in plan}
    acc = z_tile
    snaps = {}
    for j in range(1, max(wanted)):
        acc = acc + load_shifted(j)
        if j + 1 in wanted:
            snaps[j + 1] = acc
    if len(plan) == 1:
        w = plan[0][0]
        win = snaps[w]
        cnt = jnp.minimum(float(w), pos1)
    else:
        (w_lo, edge), (w_hi, _) = plan
        lane = lax.broadcasted_iota(jnp.int32, (1,) * (z_tile.ndim - 1) + (LANES,), z_tile.ndim - 1)
        in_lo = lane + ct * LANES < edge
        win = jnp.where(in_lo, snaps[w_lo], snaps[w_hi])
        cnt = jnp.minimum(jnp.where(in_lo, float(w_lo), float(w_hi)), pos1)
    return win / cnt - z_tile


def _pool_prompt_kernel(z_ref, wbd_ref, scale_ref, mix_ref, state_ref, zc_s, d_s):
    i = pl.program_id(1)
    tile = z_ref.shape[0]

    @pl.when(i == 0)
    def _start():
        zc_s[0:POOL_HALO, :] = jnp.zeros((POOL_HALO, MIX_W), F32)

    zc_s[POOL_HALO:POOL_HALO + tile, :] = z_ref[...]
    pos1 = (i * tile + lax.broadcasted_iota(jnp.int32, (tile, 1), 0) + 1).astype(F32)
    for ct in range(MIX_W // LANES):
        ln = slice(ct * LANES, (ct + 1) * LANES)
        diff = _pool_diff_tile(
            lambda j: zc_s[POOL_HALO - j:POOL_HALO - j + tile, ln], z_ref[:, ln], pos1, ct)
        d_s[:, ln] = diff.astype(BF16)
    mix_ref[...] = jnp.dot(d_s[...], wbd_ref[...], preferred_element_type=F32) * scale_ref[...]
    last = z_ref[tile - POOL_HALO:tile, :]
    state_ref[...] = last
    zc_s[0:POOL_HALO, :] = last


def _pool_prompt(proj, nb, seq, wbd_bf16, scale, tile=TOK_TILE):
    nt = seq // tile
    assert seq % tile == 0 and tile >= POOL_HALO
    return pl.pallas_call(
        _pool_prompt_kernel,
        grid=(nb, nt),
        in_specs=[
            pl.BlockSpec((tile, MIX_W), lambda n, i: (n * nt + i, 0)),
            pl.BlockSpec((MIX_W, MIX_W), lambda n, i: (0, 0)),
            pl.BlockSpec((1, MIX_W), lambda n, i: (0, 0)),
        ],
        out_specs=[
            pl.BlockSpec((tile, MIX_W), lambda n, i: (n * nt + i, 0)),
            pl.BlockSpec((None, POOL_HALO, MIX_W), lambda n, i: (n, 0, 0)),
        ],
        out_shape=[
            jax.ShapeDtypeStruct((nb * seq, MIX_W), F32),
            jax.ShapeDtypeStruct((nb, POOL_HALO, MIX_W), F32),
        ],
        scratch_shapes=[
            pltpu.VMEM((POOL_HALO + tile, MIX_W), F32),
            pltpu.VMEM((tile, MIX_W), BF16),
        ],
        compiler_params=pltpu.CompilerParams(
            dimension_semantics=("arbitrary", "arbitrary"), vmem_limit_bytes=VMEM_LIMIT),
        name="pool_prompt",
    )(proj, wbd_bf16, scale.reshape(1, MIX_W))


def _pool_sample_kernel(z_ref, st_ref, wbd_ref, scale_ref, mix_ref, nst_ref, zc_s, d_s, *, pos0):
    grp, n_state, _ = st_ref.shape
    t = z_ref.shape[0] // grp
    zc_s[:, POOL_HALO - n_state:POOL_HALO, :] = st_ref[...]
    zc_s[:, POOL_HALO:POOL_HALO + t, :] = z_ref[...].reshape(grp, t, MIX_W)
    pos1 = (pos0 + lax.broadcasted_iota(jnp.int32, (1, t, 1), 1) + 1).astype(F32)
    for ct in range(MIX_W // LANES):
        ln = slice(ct * LANES, (ct + 1) * LANES)
        diff = _pool_diff_tile(
            lambda j: zc_s[:, POOL_HALO - j:POOL_HALO - j + t, ln],
            zc_s[:, POOL_HALO:POOL_HALO + t, ln], pos1, ct)
        d_s[:, ln] = diff.reshape(grp * t, LANES).astype(BF16)
    mix_ref[...] = jnp.dot(d_s[...], wbd_ref[...], preferred_element_type=F32) * scale_ref[...]
    nst_ref[...] = zc_s[:, POOL_HALO + t - n_state:POOL_HALO + t, :]


def _pool_sample(proj, nb, t, state, wbd_bf16, scale, pos0, grp=SAMPLE_GROUP):
    n_state = state.shape[1]
    assert nb % grp == 0 and t % SUBLANES == 0
    assert max(POOL_WINDOWS) - 1 <= n_state <= POOL_HALO - 1
    return pl.pallas_call(
        functools.partial(_pool_sample_kernel, pos0=pos0),
        grid=(nb // grp,),
        in_specs=[
            pl.BlockSpec((grp * t, MIX_W), lambda i: (i, 0)),
            pl.BlockSpec((grp, n_state, MIX_W), lambda i: (i, 0, 0)),
            pl.BlockSpec((MIX_W, MIX_W), lambda i: (0, 0)),
            pl.BlockSpec((1, MIX_W), lambda i: (0, 0)),
        ],
        out_specs=[
            pl.BlockSpec((grp * t, MIX_W), lambda i: (i, 0)),
            pl.BlockSpec((grp, n_state, MIX_W), lambda i: (i, 0, 0)),
        ],
        out_shape=[
            jax.ShapeDtypeStruct((nb * t, MIX_W), F32),
            jax.ShapeDtypeStruct((nb, n_state, MIX_W), F32),
        ],
        scratch_shapes=[
            pltpu.VMEM((grp, POOL_HALO + t, MIX_W), F32),
            pltpu.VMEM((grp * t, MIX_W), BF16),
        ],
        compiler_params=pltpu.CompilerParams(
            dimension_semantics=("parallel",), vmem_limit_bytes=VMEM_LIMIT),
        name="pool_sample",
    )(proj, state, wbd_bf16, scale.reshape(1, MIX_W))


def _mem_out_kernel(x_ref, mix_ref, qm_ref, mk_ref, mv_ref, wmix_ref, wmem_ref, o_ref):
    grp = mk_ref.shape[0]
    rows = x_ref.shape[0]
    tq = rows // grp
    acc = x_ref[...] + jnp.dot(mix_ref[...].astype(BF16), wmix_ref[...],
                               preferred_element_type=F32)
    heads = []
    for h in range(MEM_HEADS):
        cols = slice(h * MEM_HD, (h + 1) * MEM_HD)
        q = qm_ref[:, cols].astype(BF16).reshape(grp, tq, MEM_HD)
        k = mk_ref[:, :, cols].astype(BF16)
        v = mv_ref[:, :, cols].astype(BF16)
        s = jnp.einsum('bqd,bkd->bqk', q, k, preferred_element_type=F32) * (MEM_HD ** -0.5)
        e = jnp.exp(s - jnp.max(s, axis=-1, keepdims=True))
        p = e / jnp.sum(e, axis=-1, keepdims=True)
        o = jnp.einsum('bqk,bkd->bqd', p.astype(BF16), v, preferred_element_type=F32)
        heads.append(o.reshape(rows, MEM_HD).astype(BF16))
    o_ref[...] = acc + jnp.dot(jnp.concatenate(heads, axis=-1), wmem_ref[...],
                               preferred_element_type=F32)


def _mem_out(x, mix, proj, qm_col_blk, mem_k, mem_v, wmix_bf16, wmem_bf16, grp, tq):
    rows, d = x.shape
    kmix = mix.shape[1]
    n_seq, n_mem, _ = mem_k.shape
    step = grp * tq
    per_seq = rows // n_seq
    assert rows % step == 0 and tq % SUBLANES == 0
    assert (grp == 1 and per_seq % tq == 0) or per_seq == tq
    tiles_per_seq = per_seq // tq
    seq_blk = (lambda i: i // tiles_per_seq) if grp == 1 else (lambda i: i)
    return pl.pallas_call(
        _mem_out_kernel,
        grid=(rows // step,),
        in_specs=[
            pl.BlockSpec((step, d), lambda i: (i, 0)),
            pl.BlockSpec((step, kmix), lambda i: (i, 0)),
            pl.BlockSpec((step, MEM_W), lambda i: (i, qm_col_blk)),
            pl.BlockSpec((grp, n_mem, MEM_W), lambda i: (seq_blk(i), 0, 0)),
            pl.BlockSpec((grp, n_mem, MEM_W), lambda i: (seq_blk(i), 0, 0)),
            pl.BlockSpec((kmix, d), lambda i: (0, 0)),
            pl.BlockSpec((MEM_W, d), lambda i: (0, 0)),
        ],
        out_specs=pl.BlockSpec((step, d), lambda i: (i, 0)),
        out_shape=jax.ShapeDtypeStruct((rows, d), F32),
        compiler_params=pltpu.CompilerParams(
            dimension_semantics=("parallel",), vmem_limit_bytes=VMEM_LIMIT),
        name="mem_out",
    )(x, mix, proj, mem_k, mem_v, wmix_bf16, wmem_bf16)


def _dil_prompt_kernel(slope_ref, q_ref, kc_ref, kp_ref, vc_ref, vp_ref, o_ref, og_s, lse_s):
    win = pl.program_id(1)
    rows = q_ref.shape[0]
    blk = DIL_BLOCK
    slope = slope_ref[...]
    ri = lax.broadcasted_iota(jnp.int32, (blk, blk), 0)
    ci = lax.broadcasted_iota(jnp.int32, (blk, blk), 1)
    diff = (ri - ci).astype(F32)
    own_ok = (ri >= ci)[None]
    prev_ok = (ci >= ri)[None]
    not_first = win > 0
    scale = B_HD ** -0.5

    for g, (w, d) in enumerate(DIL_PATTERNS):
        per_res = rows // (blk * d)
        assert w // d == blk and rows % (blk * d) == 0 and per_res & (per_res - 1) == 0

        def rows_of(r, j):
            return pl.ds(j * blk * d + r, blk, stride=d) if d > 1 else pl.ds(j * blk, blk)

        own = [rows_of(r, j) for r in range(d) for j in range(per_res)]
        prv = [(kc_ref, vc_ref, rows_of(r, j - 1)) if j > 0
               else (kp_ref, vp_ref, rows_of(r, per_res - 1))
               for r in range(d) for j in range(per_res)]
        q = jnp.stack([q_ref[o, :] for o in own]).astype(BF16)
        k_own = jnp.stack([kc_ref[o, :] for o in own]).astype(BF16)
        v_own = jnp.stack([vc_ref[o, :] for o in own]).astype(BF16)
        k_prev = jnp.stack([kr[o, :] for kr, _, o in prv]).astype(BF16)
        v_prev = jnp.stack([vr[o, :] for _, vr, o in prv]).astype(BF16)
        bias_own = (-(slope * float(d)) * diff)[None]
        bias_prev = (-(slope * float(d)) * (diff + float(blk)))[None]
        s_own = jnp.einsum('bqd,bkd->bqk', q, k_own, preferred_element_type=F32) * scale + bias_own
        s_own = jnp.where(own_ok, s_own, NEG)
        s_prev = jnp.einsum('bqd,bkd->bqk', q, k_prev, preferred_element_type=F32) * scale + bias_prev
        bi = lax.broadcasted_iota(jnp.int32, (len(own), blk, blk), 0)
        has_prev = jnp.logical_or((bi & (per_res - 1)) != 0, not_first)
        s_prev = jnp.where(jnp.logical_and(prev_ok, has_prev), s_prev, NEG)
        m = jnp.maximum(jnp.max(s_own, axis=-1, keepdims=True),
                        jnp.max(s_prev, axis=-1, keepdims=True))
        e_own = jnp.exp(s_own - m)
        e_prev = jnp.exp(s_prev - m)
        den = jnp.sum(e_own, axis=-1, keepdims=True) + jnp.sum(e_prev, axis=-1, keepdims=True)
        o = (jnp.einsum('bqk,bkd->bqd', (e_own / den).astype(BF16), v_own,
                        preferred_element_type=F32)
             + jnp.einsum('bqk,bkd->bqd', (e_prev / den).astype(BF16), v_prev,
                          preferred_element_type=F32))
        lse = jnp.broadcast_to(m + jnp.log(den), o.shape)
        for b, rows_b in enumerate(own):
            og_s[g, rows_b, :] = o[b]
            lse_s[g, rows_b, :] = lse[b]

    top = jnp.maximum(jnp.maximum(lse_s[0], lse_s[1]), lse_s[2])
    wts = [jnp.exp(lse_s[g] - top) for g in range(len(DIL_PATTERNS))]
    tot = wts[0] + wts[1] + wts[2]
    o_ref[...] = ((wts[0] / tot) * og_s[0] + (wts[1] / tot) * og_s[1] + (wts[2] / tot) * og_s[2])


def _dil_prompt(proj_pad, kv_pad, nb, seq, slopes_b):
    win = DIL_WIN
    nw = seq // win
    assert seq % win == 0 and len(DIL_PATTERNS) == 3
    cur = lambda col0: (lambda n, w, h: (n * nw + w, col0 + h))
    prev = lambda col0: (lambda n, w, h: (n * nw + jnp.maximum(w - 1, 0), col0 + h))
    return pl.pallas_call(
        _dil_prompt_kernel,
        grid=(nb, nw, B_HEADS),
        in_specs=[
            pl.BlockSpec((None, 1, LANES), lambda n, w, h: (h, 0, 0)),
            pl.BlockSpec((win, LANES), cur(0)),
            pl.BlockSpec((win, LANES), cur(0)),
            pl.BlockSpec((win, LANES), prev(0)),
            pl.BlockSpec((win, LANES), cur(B_HEADS)),
            pl.BlockSpec((win, LANES), prev(B_HEADS)),
        ],
        out_specs=pl.BlockSpec((win, LANES), cur(0)),
        out_shape=jax.ShapeDtypeStruct((nb * seq, B_HEADS * LANES), F32),
        scratch_shapes=[
            pltpu.VMEM((len(DIL_PATTERNS), win, LANES), F32),
            pltpu.VMEM((len(DIL_PATTERNS), win, LANES), F32),
        ],
        compiler_params=pltpu.CompilerParams(
            dimension_semantics=("parallel", "parallel", "parallel"),
            vmem_limit_bytes=VMEM_LIMIT),
        name="dil_prompt",
    )(slopes_b, proj_pad, kv_pad, kv_pad, kv_pad, kv_pad)


def _dil_sample_kernel(q_ref, kc_ref, vc_ref, kn_ref, vn_ref, bc_ref, bn_ref, hm_ref, o_ref,
                       knp_s, vnp_s, pc_s, *, row_lo):
    t = q_ref.shape[0]
    nt = (((1,), (1,)), ((), ()))
    scale = B_HD ** -0.5
    hm = hm_ref[...]
    qbd = (jnp.concatenate([q_ref[...]] * B_HEADS, axis=0) * hm).astype(BF16)
    knp_s[...] = jnp.zeros_like(knp_s)
    vnp_s[...] = jnp.zeros_like(vnp_s)
    knp_s[0:t, :] = kn_ref[...]
    vnp_s[0:t, :] = vn_ref[...]
    s_c = lax.dot_general(qbd, kc_ref[...].astype(BF16), nt, preferred_element_type=F32) * scale
    s_n = lax.dot_general(qbd, knp_s[...].astype(BF16), nt, preferred_element_type=F32) * scale
    stats = []
    for g in range(len(DIL_PATTERNS)):
        lo = row_lo[g]
        bc = bc_ref[g, :, lo:]
        bn = bn_ref[g]
        sc = jnp.where(bc > 0.5 * NEG, s_c[:, lo:] + bc, NEG)
        sn = jnp.where(bn > 0.5 * NEG, s_n + bn, NEG)
        m = jnp.maximum(jnp.max(sc, axis=-1, keepdims=True), jnp.max(sn, axis=-1, keepdims=True))
        ec = jnp.exp(sc - m)
        en = jnp.exp(sn - m)
        den = jnp.sum(ec, axis=-1, keepdims=True) + jnp.sum(en, axis=-1, keepdims=True)
        stats.append((ec / den, en / den, m + jnp.log(den)))
    top = jnp.maximum(jnp.maximum(stats[0][2], stats[1][2]), stats[2][2])
    wts = [jnp.exp(st[2] - top) for st in stats]
    tot = wts[0] + wts[1] + wts[2]
    pc_s[...] = jnp.zeros_like(pc_s)
    pn = jnp.zeros((B_HEADS * t, knp_s.shape[0]), F32)
    for g in range(len(DIL_PATTERNS)):
        lo = row_lo[g]
        alpha = wts[g] / tot
        pc_s[:, lo:] += alpha * stats[g][0]
        pn = pn + alpha * stats[g][1]
    o = (jnp.dot(pc_s[...].astype(BF16), vc_ref[...].astype(BF16), preferred_element_type=F32)
         + jnp.dot(pn.astype(BF16), vnp_s[...].astype(BF16), preferred_element_type=F32))
    o_ref[...] = jnp.sum((o * hm).reshape(B_HEADS, t, MIX_W), axis=0)


def _dil_sample_tables(past, t):
    slopes = np.asarray([2.0 ** (-8.0 * (h + 1) / B_HEADS) for h in range(B_HEADS)], np.float32)
    qpos = past + np.arange(t)
    col = np.arange(past + LANES)
    exists = col < past + t
    delta = qpos[:, None] - col[None, :]
    bias = np.full((len(DIL_PATTERNS), B_HEADS, t, past + LANES), NEG, np.float32)
    col_lo = []
    for g, (w, d) in enumerate(DIL_PATTERNS):
        ok = (delta >= 0) & (delta <= w) & (delta % d == 0) & exists[None, :]
        vals = -slopes[:, None, None] * delta[None].astype(np.float32)
        bias[g] = np.where(ok[None], vals, np.float32(NEG))
        col_lo.append(max(0, (past - w) // LANES * LANES))
    bias = bias.reshape(len(DIL_PATTERNS), B_HEADS * t, past + LANES)
    hm = (np.arange(MIX_W)[None, :] // B_HD == np.arange(B_HEADS * t)[:, None] // t)
    return bias[:, :, :past], bias[:, :, past:], hm.astype(np.float32), tuple(col_lo)


def _dil_sample(proj, kv_new, cache_k, cache_v):
    nb, past, _ = cache_k.shape
    t = proj.shape[0] // nb
    assert t % SUBLANES == 0 and t <= LANES and past % LANES == 0 and len(DIL_PATTERNS) == 3
    bias_c, bias_n, hm, col_lo = _dil_sample_tables(past, t)
    ng = len(DIL_PATTERNS)
    return pl.pallas_call(
        functools.partial(_dil_sample_kernel, row_lo=col_lo),
        grid=(nb,),
        in_specs=[
            pl.BlockSpec((t, MIX_W), lambda b: (b, 0)),
            pl.BlockSpec((None, past, MIX_W), lambda b: (b, 0, 0)),
            pl.BlockSpec((None, past, MIX_W), lambda b: (b, 0, 0)),
            pl.BlockSpec((t, MIX_W), lambda b: (b, 0)),
            pl.BlockSpec((t, MIX_W), lambda b: (b, 1)),
            pl.BlockSpec((ng, B_HEADS * t, past), lambda b: (0, 0, 0)),
            pl.BlockSpec((ng, B_HEADS * t, LANES), lambda b: (0, 0, 0)),
            pl.BlockSpec((B_HEADS * t, MIX_W), lambda b: (0, 0)),
        ],
        out_specs=pl.BlockSpec((t, MIX_W), lambda b: (b, 0)),
        out_shape=jax.ShapeDtypeStruct((nb * t, MIX_W), F32),
        scratch_shapes=[
            pltpu.VMEM((LANES, MIX_W), F32),
            pltpu.VMEM((LANES, MIX_W), F32),
            pltpu.VMEM((B_HEADS * t, past), F32),
        ],
        compiler_params=pltpu.CompilerParams(
            dimension_semantics=("parallel",), vmem_limit_bytes=VMEM_LIMIT),
        name="dil_sample",
    )(proj, cache_k, cache_v, kv_new, kv_new, jnp.asarray(bias_c), jnp.asarray(bias_n),
      jnp.asarray(hm))


def _pad_head_cols(w):
    d = w.shape[0]
    w = jnp.pad(w.reshape(d, B_HEADS, B_HD), ((0, 0), (0, 0), (0, LANES - B_HD)))
    return w.reshape(d, B_HEADS * LANES)


def _unpad_heads(a):
    return a.reshape(a.shape[:-1] + (B_HEADS, LANES))[..., :B_HD]


def kernel(x_prompt, x_sample, state_pool, cache_win_k, cache_win_v, cache_mem_k, cache_mem_v,
           mem_prompt, norm_mix, w_in, pool_w, pool_scale, norm_mem, w_mem_kv, w_out,
           norm_kv, w_kv, norm_ffn, peer_wq, peer_subkeys, peer_u, peer_v, norm_final):
    nb_p, seq, d = x_prompt.shape
    nb_s, dec_seq, _ = x_sample.shape
    n_mem = mem_prompt.shape[1]
    past = cache_win_k.shape[1]
    tp = nb_p * seq
    ts = nb_s * dec_seq

    mem_flat = mem_prompt.reshape(nb_p * n_mem, d)
    mkv = [_norm_matmul(mem_flat, norm_mem[l], w_mem_kv[l].astype(BF16)) for l in range(DEPTH)]
    mem_k_p = jnp.stack([m[:, :MEM_W].reshape(nb_p, n_mem, MEM_W) for m in mkv], axis=0)
    mem_v_p = jnp.stack([m[:, MEM_W:].reshape(nb_p, n_mem, MEM_W) for m in mkv], axis=0)
    mem_k_s = cache_mem_k.reshape(DEPTH, nb_s, n_mem, MEM_W)
    mem_v_s = cache_mem_v.reshape(DEPTH, nb_s, n_mem, MEM_W)

    slopes_b = jnp.broadcast_to(
        jnp.asarray([2.0 ** (-8.0 * (h + 1) / B_HEADS) for h in range(B_HEADS)], F32)[:, None, None],
        (B_HEADS, 1, LANES))
    cache_k = cache_win_k.reshape(nb_s, past, MIX_W)
    cache_v = cache_win_v.reshape(nb_s, past, MIX_W)

    xp = x_prompt.reshape(tp, d)
    xs = x_sample.reshape(ts, d)
    pool_p, pool_s = [], []
    kv_p = kv_s = None
    for l in range(DEPTH):
        w_in_l = w_in[l].astype(BF16)
        w_out_l = w_out[l].astype(BF16)
        proj_s = _norm_matmul(xs, norm_mix[l], w_in_l)
        if l < N_A:
            proj_p = _norm_matmul(xp, norm_mix[l], w_in_l)
            wbd = jax.scipy.linalg.block_diag(*[pool_w[l, g] for g in range(POOL_GROUPS)]).astype(BF16)
            mix_p, last_p = _pool_prompt(proj_p, nb_p, seq, wbd, pool_scale[l])
            pool_p.append(last_p[:, POOL_HALO - POOL_STATE:])
            mix_s, new_state = _pool_sample(proj_s, nb_s, dec_seq, state_pool[l], wbd,
                                            pool_scale[l], PAST_LEN)
            pool_s.append(new_state)
            wmix_p = w_out_l[:MIX_W]
            qm_blk_p = MIX_W // MEM_W
        else:
            w_in_pad = jnp.concatenate([_pad_head_cols(w_in[l][:, :MIX_W]), w_in[l][:, MIX_W:]],
                                       axis=1).astype(BF16)
            proj_p = _norm_matmul(xp, norm_mix[l], w_in_pad)
            mix_p = _dil_prompt(proj_p, kv_p, nb_p, seq, slopes_b)
            mix_s = _dil_sample(proj_s, kv_s, cache_k, cache_v)
            wmix_p = _pad_head_cols(w_out[l][:MIX_W].T).T.astype(BF16)
            qm_blk_p = B_HEADS * LANES // MEM_W
        xp = _mem_out(xp, mix_p, proj_p, qm_blk_p, mem_k_p[l], mem_v_p[l], wmix_p,
                      w_out_l[MIX_W:], 1, TOK_TILE)
        xs = _mem_out(xs, mix_s, proj_s, MIX_W // MEM_W, mem_k_s[l], mem_v_s[l], w_out_l[:MIX_W],
                      w_out_l[MIX_W:], SAMPLE_GROUP, dec_seq)
        peer_w = (norm_ffn[l], peer_wq[l].T.astype(BF16), peer_subkeys[l].astype(BF16),
                  peer_u[l].astype(BF16), peer_v[l].T.astype(BF16))
        xp = _peer(xp, *peer_w)
        xs = _peer(xs, *peer_w)
        if l == N_A - 1:
            w_kv_pad = jnp.concatenate([_pad_head_cols(w_kv[:, :MIX_W]),
                                        _pad_head_cols(w_kv[:, MIX_W:])], axis=1).astype(BF16)
            kv_p = _norm_matmul(xp, norm_kv, w_kv_pad)
            kv_s = _norm_matmul(xs, norm_kv, w_kv.astype(BF16))
    y_p = _final_norm(xp, norm_final)
    y_s = _final_norm(xs, norm_final)
    keep = min(WINDOW_MAX, seq)
    kv_p4 = kv_p.reshape(nb_p, seq, 2 * B_HEADS * LANES)[:, seq - keep:]
    return (y_p.reshape(nb_p, seq, d), y_s.reshape(nb_s, dec_seq, d),
            jnp.stack(pool_p, axis=0),
            _unpad_heads(kv_p4[..., :B_HEADS * LANES]), _unpad_heads(kv_p4[..., B_HEADS * LANES:]),
            mem_k_p.reshape(DEPTH, nb_p, n_mem, MEM_HEADS, MEM_HD),
            mem_v_p.reshape(DEPTH, nb_p, n_mem, MEM_HEADS, MEM_HD),
            jnp.stack(pool_s, axis=0),
            kv_s[:, :MIX_W].reshape(nb_s, dec_seq, B_HEADS, B_HD),
            kv_s[:, MIX_W:].reshape(nb_s, dec_seq, B_HEADS, B_HD))
```
